```python
import math
import jax, jax.numpy as jnp
from jax import lax
import numpy as np

D_MODEL = 4096
BATCH = 4
SEQ = 2048
DEPTH = 4
DEC_BATCH = 32
DEC_SEQ = 1
PAST_LEN = 8192
PAGE_SIZE = 128

GROUP_WIDTH = D_MODEL // 4
MIX_WIDTH = 4 * GROUP_WIDTH
RET_HEADS = 8
RET_HEAD_DIM = GROUP_WIDTH // RET_HEADS
RET_CHUNK = 128
ATT_HEAD_DIM = 64
ATT_HEADS = GROUP_WIDTH // ATT_HEAD_DIM
ATT_KV_HEADS = 2
ATT_GROUP = ATT_HEADS // ATT_KV_HEADS
WINDOW = 128
ATT_BLOCK = 128
CONV_CH = GROUP_WIDTH
CONV_WIDTH = 31
POOL_CH = GROUP_WIDTH
POOL_WINDOWS = (2, 4, 8, 16)
POOL_GROUP = POOL_CH // len(POOL_WINDOWS)
POOL_MAX = 16
D_FF = ((8 * D_MODEL // 3 + 255) // 256) * 256
FFN_CONV_WIDTH = 3
NORM_EPS = 1e-6

IN_WIDTH = 4 * GROUP_WIDTH + (ATT_HEADS + 2 * ATT_KV_HEADS) * ATT_HEAD_DIM + 2 * CONV_CH + POOL_CH

kernel_name = 'hybrid_parallel_heads_decode_step'


def _rms_norm(x, g):
    xf = x.astype(jnp.float32)
    y = xf * lax.rsqrt(jnp.mean(xf * xf, axis=-1, keepdims=True) + NORM_EPS)
    return (y * g.astype(jnp.float32)).astype(x.dtype)


def _layer_norm(x, g, b):
    xf = x.astype(jnp.float32)
    mu = jnp.mean(xf, axis=-1, keepdims=True)
    xc = xf - mu
    y = xc * lax.rsqrt(jnp.mean(xc * xc, axis=-1, keepdims=True) + NORM_EPS)
    return (y * g.astype(jnp.float32) + b.astype(jnp.float32)).astype(x.dtype)


def _split_columns(h):
    sizes = (GROUP_WIDTH, GROUP_WIDTH, GROUP_WIDTH, GROUP_WIDTH,
             ATT_HEADS * ATT_HEAD_DIM, ATT_KV_HEADS * ATT_HEAD_DIM, ATT_KV_HEADS * ATT_HEAD_DIM,
             2 * CONV_CH, POOL_CH)
    out, start = [], 0
    for s in sizes:
        out.append(h[..., start:start + s])
        start += s
    return out


def _causal_dwconv(x, past, w, b):
    width = w.shape[0]
    xp = jnp.concatenate([past.astype(x.dtype), x], axis=1)
    y = lax.conv_general_dilated(xp, w.astype(x.dtype)[:, None, :], window_strides=(1,), padding='VALID',
                                 dimension_numbers=('NWC', 'WIO', 'NWC'), feature_group_count=x.shape[-1])
    return y + b.astype(x.dtype), xp[:, xp.shape[1] - (width - 1):]


def _retention(q, k, v, s0):
    B, L, H, dk = q.shape
    dv = v.shape[-1]
    C = RET_CHUNK if L % RET_CHUNK == 0 else L
    n = L // C
    log_g = jnp.log1p(-jnp.exp2(-5.0 - jnp.arange(H, dtype=jnp.float32)))
    idx = jnp.arange(C, dtype=jnp.float32)
    rel = idx[:, None] - idx[None, :]
    dmask = jnp.where(rel >= 0, jnp.exp(log_g[:, None, None] * jnp.maximum(rel, 0.0)), 0.0)
    q_dec = jnp.exp(log_g[:, None] * (idx + 1.0))[None, :, :, None]
    k_dec = jnp.exp(log_g[:, None] * (C - 1.0 - idx))[None, :, :, None]
    c_dec = jnp.exp(log_g * C)[None, :, None, None]

    def to_chunks(a):
        return a.astype(jnp.float32).reshape(B, n, C, H, a.shape[-1]).transpose(1, 0, 3, 2, 4)

    qc, kc, vc = to_chunks(q), to_chunks(k * (dk ** -0.5)), to_chunks(v)

    def step(S, inp):
        qi, ki, vi = inp
        a = jnp.einsum('bhqd,bhkd->bhqk', qi, ki) * dmask
        o = jnp.einsum('bhqk,bhkv->bhqv', a, vi) + jnp.einsum('bhqd,bhdv->bhqv', qi, S) * q_dec
        S = S * c_dec + jnp.einsum('bhkd,bhkv->bhdv', ki * k_dec, vi)
        return S, o

    S, o = lax.scan(step, s0.astype(jnp.float32), (qc, kc, vc))
    o = o.transpose(1, 0, 3, 2, 4).reshape(B, L, H, dv)
    return o, S


def _head_group_norm(o, g):
    B, L, H, dv = o.shape
    mu = jnp.mean(o, axis=-1, keepdims=True)
    oc = o - mu
    y = oc * lax.rsqrt(jnp.mean(oc * oc, axis=-1, keepdims=True) + NORM_EPS)
    return (y * g.astype(jnp.float32).reshape(H, dv)).reshape(B, L, H * dv)


def _alibi_slopes():
    return jnp.exp2(-8.0 * jnp.arange(1, ATT_HEADS + 1, dtype=jnp.float32) / ATT_HEADS)


def _sink_attention(qb, kb, vb, dist, valid, sinks):
    B, N, Q = qb.shape[:3]
    slopes = _alibi_slopes().reshape(ATT_KV_HEADS, ATT_GROUP)
    s = jnp.einsum('bnqhgd,bnkhd->bnhgqk', qb, kb).astype(jnp.float32) * (ATT_HEAD_DIM ** -0.5)
    s = s - slopes[None, None, :, :, None, None] * dist.astype(jnp.float32)[None, :, None, None]
    s = jnp.where(valid[None, :, None, None], s, -jnp.inf)
    sink = jnp.broadcast_to(sinks.astype(jnp.float32).reshape(ATT_KV_HEADS, ATT_GROUP)[None, None, :, :, None, None],
                            s.shape[:-1] + (1,))
    p = jax.nn.softmax(jnp.concatenate([s, sink], axis=-1), axis=-1)[..., :-1]
    o = jnp.einsum('bnhgqk,bnkhd->bnqhgd', p.astype(vb.dtype), vb)
    return o.reshape(B, N * Q, ATT_HEADS * ATT_HEAD_DIM)


def _swa_prompt(q, k, v, sinks):
    B, L = q.shape[:2]
    nb = L // ATT_BLOCK
    qb = q.reshape(B, nb, ATT_BLOCK, ATT_KV_HEADS, ATT_GROUP, ATT_HEAD_DIM)
    kb = k.reshape(B, nb, ATT_BLOCK, ATT_KV_HEADS, ATT_HEAD_DIM)
    vb = v.reshape(B, nb, ATT_BLOCK, ATT_KV_HEADS, ATT_HEAD_DIM)
    kk = jnp.concatenate([jnp.concatenate([jnp.zeros_like(kb[:, :1]), kb[:, :-1]], axis=1), kb], axis=2)
    vv = jnp.concatenate([jnp.concatenate([jnp.zeros_like(vb[:, :1]), vb[:, :-1]], axis=1), vb], axis=2)
    blk = jnp.arange(nb)[:, None] * ATT_BLOCK
    q_pos = blk + jnp.arange(ATT_BLOCK)[None, :]
    k_pos = blk - ATT_BLOCK + jnp.arange(2 * ATT_BLOCK)[None, :]
    dist = q_pos[:, :, None] - k_pos[:, None, :]
    valid = (dist >= 0) & (dist <= WINDOW) & (k_pos[:, None, :] >= 0)
    return _sink_attention(qb, kk, vv, dist, valid, sinks)


def _swa_sample(q, k_all, v_all, sinks):
    B, L = q.shape[:2]
    K = k_all.shape[1]
    qb = q.reshape(B, 1, L, ATT_KV_HEADS, ATT_GROUP, ATT_HEAD_DIM)
    q_pos = PAST_LEN + jnp.arange(L)
    k_pos = PAST_LEN - WINDOW + jnp.arange(K)
    dist = (q_pos[:, None] - k_pos[None, :])[None]
    valid = (dist >= 0) & (dist <= WINDOW)
    return _sink_attention(qb, k_all[:, None], v_all[:, None], dist, valid, sinks)


def _conformer_conv(h2, past, dw_w, dw_b, ln_g, ln_b, pw2):
    a, gate = h2[..., :CONV_CH], h2[..., CONV_CH:]
    u = a * jax.nn.sigmoid(gate)
    c, buf = _causal_dwconv(u, past, dw_w, dw_b)
    c = jax.nn.silu(_layer_norm(c, ln_g, ln_b))
    return c @ pw2, buf


def _pool_mixer(x, past, pool_w, pool_scale, n_past):
    B, L, _ = x.shape
    P = POOL_MAX - 1
    full = jnp.concatenate([past.astype(x.dtype), x], axis=1)
    cs = jnp.concatenate([jnp.zeros((B, 1, POOL_CH), jnp.float32),
                          jnp.cumsum(full.astype(jnp.float32), axis=1)], axis=1)
    t = jnp.arange(L)
    xf = x.astype(jnp.float32)
    outs = []
    for gi, w in enumerate(POOL_WINDOWS):
        sl = slice(gi * POOL_GROUP, (gi + 1) * POOL_GROUP)
        hi = cs[:, P + 1:P + 1 + L, sl]
        lo = cs[:, P + 1 - w:P + 1 - w + L, sl]
        cnt = jnp.minimum(w, t + 1 + n_past).astype(jnp.float32)[None, :, None]
        outs.append((hi - lo) / cnt - xf[..., sl])
    d = jnp.stack(outs, axis=2).astype(x.dtype)
    y = jnp.einsum('blgc,gcd->blgd', d, pool_w).reshape(B, L, POOL_CH)
    return y * pool_scale, full[:, full.shape[1] - P:]


def _conv_ffn(xn, past, up, dw_w, dw_b, down):
    h = xn @ up
    gpre, val = h[..., :D_FF], h[..., D_FF:]
    c, buf = _causal_dwconv(gpre, past, dw_w, dw_b)
    return (jax.nn.silu(c) * val) @ down, buf


def _layer(x, past, w, is_prompt):
    (norm1_g, w_in, ret_gn_g, q_norm_g, k_norm_g, att_sinks, conv_dw_w, conv_dw_b, conv_ln_g, conv_ln_b,
     conv_pw2, pool_w, pool_scale, w_out, norm2_g, ffn_up, ffn_dw_w, ffn_dw_b, ffn_down) = w
    ret_s0, k_past, v_past, conv_past, pool_past, ffn_past = past
    B, L, _ = x.shape
    h = _rms_norm(x, norm1_g) @ w_in
    rq, rk, rv, rg, aq, ak, av, ch, pin = _split_columns(h)
    ro, ret_s = _retention(rq.reshape(B, L, RET_HEADS, RET_HEAD_DIM), rk.reshape(B, L, RET_HEADS, RET_HEAD_DIM),
                           rv.reshape(B, L, RET_HEADS, RET_HEAD_DIM), ret_s0)
    ro = jax.nn.silu(rg) * _head_group_norm(ro, ret_gn_g).astype(x.dtype)
    q = _rms_norm(aq.reshape(B, L, ATT_HEADS, ATT_HEAD_DIM), q_norm_g)
    k = _rms_norm(ak.reshape(B, L, ATT_KV_HEADS, ATT_HEAD_DIM), k_norm_g)
    v = av.reshape(B, L, ATT_KV_HEADS, ATT_HEAD_DIM)
    if is_prompt:
        ao = _swa_prompt(q, k, v, att_sinks)
        k_all, v_all = k, v
    else:
        k_all = jnp.concatenate([k_past.astype(k.dtype), k], axis=1)
        v_all = jnp.concatenate([v_past.astype(v.dtype), v], axis=1)
        ao = _swa_sample(q, k_all, v_all, att_sinks)
    k_win = k_all[:, k_all.shape[1] - WINDOW:]
    v_win = v_all[:, v_all.shape[1] - WINDOW:]
    co, conv_buf = _conformer_conv(ch, conv_past, conv_dw_w, conv_dw_b, conv_ln_g, conv_ln_b, conv_pw2)
    po, pool_buf = _pool_mixer(pin, pool_past, pool_w, pool_scale, 0 if is_prompt else POOL_MAX - 1)
    x = x + jnp.concatenate([ro, ao, co, po], axis=-1) @ w_out
    f, ffn_buf = _conv_ffn(_rms_norm(x, norm2_g), ffn_past, ffn_up, ffn_dw_w, ffn_dw_b, ffn_down)
    x = x + f
    return x, ret_s, k_win, v_win, conv_buf, pool_buf, ffn_buf


def setup_inputs(seed: int = 0) -> dict:
    key = jax.random.key(seed)
    ks = jax.random.split(key, 32)
    f32 = jnp.float32

    def nrm(k, shape, scale=1.0):
        return jax.random.normal(k, shape, f32) * scale

    def gain(k, shape, noise=0.02):
        return 1.0 + noise * jax.random.normal(k, shape, f32)

    return {
        'x_prompt': nrm(ks[0], (BATCH, SEQ, D_MODEL)),
        'x_sample': nrm(ks[1], (DEC_BATCH, DEC_SEQ, D_MODEL)),
        'state_ret': nrm(ks[2], (DEPTH, DEC_BATCH, RET_HEADS, RET_HEAD_DIM, RET_HEAD_DIM)),
        'cache_k_win': nrm(ks[3], (DEPTH, DEC_BATCH, WINDOW, ATT_KV_HEADS, ATT_HEAD_DIM)),
        'cache_v_win': nrm(ks[4], (DEPTH, DEC_BATCH, WINDOW, ATT_KV_HEADS, ATT_HEAD_DIM)),
        'state_conv': nrm(ks[5], (DEPTH, DEC_BATCH, CONV_WIDTH - 1, CONV_CH)),
        'state_pool': nrm(ks[6], (DEPTH, DEC_BATCH, POOL_MAX - 1, POOL_CH)),
        'state_ffn': nrm(ks[7], (DEPTH, DEC_BATCH, FFN_CONV_WIDTH - 1, D_FF)),
        'norm1_g': gain(ks[8], (DEPTH, D_MODEL)),
        'w_in': nrm(ks[9], (DEPTH, D_MODEL, IN_WIDTH), D_MODEL ** -0.5),
        'ret_gn_g': gain(ks[10], (DEPTH, GROUP_WIDTH)),
        'q_norm_g': gain(ks[11], (DEPTH, ATT_HEAD_DIM)),
        'k_norm_g': gain(ks[12], (DEPTH, ATT_HEAD_DIM)),
        'att_sinks': nrm(ks[13], (DEPTH, ATT_HEADS), 0.5),
        'conv_dw_w': nrm(ks[14], (DEPTH, CONV_WIDTH, CONV_CH), CONV_WIDTH ** -0.5),
        'conv_dw_b': nrm(ks[15], (DEPTH, CONV_CH), 0.02),
        'conv_ln_g': gain(ks[16], (DEPTH, CONV_CH)),
        'conv_ln_b': nrm(ks[17], (DEPTH, CONV_CH), 0.02),
        'conv_pw2': nrm(ks[18], (DEPTH, CONV_CH, GROUP_WIDTH), CONV_CH ** -0.5),
        'pool_w': nrm(ks[19], (DEPTH, len(POOL_WINDOWS), POOL_GROUP, POOL_GROUP), POOL_GROUP ** -0.5),
        'pool_scale': gain(ks[20], (DEPTH, POOL_CH), 0.1),
        'w_out': nrm(ks[21], (DEPTH, MIX_WIDTH, D_MODEL), MIX_WIDTH ** -0.5),
        'norm2_g': gain(ks[22], (DEPTH, D_MODEL)),
        'ffn_up': nrm(ks[23], (DEPTH, D_MODEL, 2 * D_FF), D_MODEL ** -0.5),
        'ffn_dw_w': nrm(ks[24], (DEPTH, FFN_CONV_WIDTH, D_FF), FFN_CONV_WIDTH ** -0.5),
        'ffn_dw_b': nrm(ks[25], (DEPTH, D_FF), 0.02),
        'ffn_down': nrm(ks[26], (DEPTH, D_FF, D_MODEL), D_FF ** -0.5),
    }


def reference(x_prompt, x_sample, state_ret, cache_k_win, cache_v_win, state_conv, state_pool, state_ffn,
              norm1_g, w_in, ret_gn_g, q_norm_g, k_norm_g, att_sinks, conv_dw_w, conv_dw_b, conv_ln_g, conv_ln_b,
              conv_pw2, pool_w, pool_scale, w_out, norm2_g, ffn_up, ffn_dw_w, ffn_dw_b, ffn_down):
    weights = (norm1_g, w_in, ret_gn_g, q_norm_g, k_norm_g, att_sinks, conv_dw_w, conv_dw_b, conv_ln_g, conv_ln_b,
               conv_pw2, pool_w, pool_scale, w_out, norm2_g, ffn_up, ffn_dw_w, ffn_dw_b, ffn_down)
    h = x_prompt
    Bp = x_prompt.shape[0]
    new_p = [[] for _ in range(6)]
    for i in range(DEPTH):
        w_i = tuple(a[i] for a in weights)
        past = (jnp.zeros((Bp, RET_HEADS, RET_HEAD_DIM, RET_HEAD_DIM), jnp.float32), None, None,
                jnp.zeros((Bp, CONV_WIDTH - 1, CONV_CH), x_prompt.dtype),
                jnp.zeros((Bp, POOL_MAX - 1, POOL_CH), x_prompt.dtype),
                jnp.zeros((Bp, FFN_CONV_WIDTH - 1, D_FF), x_prompt.dtype))
        h, *st = _layer(h, past, w_i, True)
        for lst, s in zip(new_p, st):
            lst.append(s)
    y_prompt = h
    h = x_sample
    new_s = [[] for _ in range(6)]
    for i in range(DEPTH):
        w_i = tuple(a[i] for a in weights)
        past = (state_ret[i], cache_k_win[i], cache_v_win[i], state_conv[i], state_pool[i], state_ffn[i])
        h, *st = _layer(h, past, w_i, False)
        for lst, s in zip(new_s, st):
            lst.append(s)
    y_sample = h
    ret_p = jnp.stack(new_p[0]).astype(state_ret.dtype)
    kwin_p = jnp.stack(new_p[1]).astype(cache_k_win.dtype)
    vwin_p = jnp.stack(new_p[2]).astype(cache_v_win.dtype)
    conv_p = jnp.stack(new_p[3]).astype(state_conv.dtype)
    pool_p = jnp.stack(new_p[4]).astype(state_pool.dtype)
    ffn_p = jnp.stack(new_p[5]).astype(state_ffn.dtype)
    ret_s = jnp.stack(new_s[0]).astype(state_ret.dtype)
    kwin_s = jnp.stack(new_s[1]).astype(cache_k_win.dtype)
    vwin_s = jnp.stack(new_s[2]).astype(cache_v_win.dtype)
    conv_s = jnp.stack(new_s[3]).astype(state_conv.dtype)
    pool_s = jnp.stack(new_s[4]).astype(state_pool.dtype)
    ffn_s = jnp.stack(new_s[5]).astype(state_ffn.dtype)
    return (y_prompt, y_sample, ret_p, kwin_p, vwin_p, conv_p, pool_p, ffn_p,
            ret_s, kwin_s, vwin_s, conv_s, pool_s, ffn_s)
```

```python
import functools

import jax
import jax.numpy as jnp
from jax import lax
from jax.experimental import pallas as pl
from jax.experimental.pallas import tpu as pltpu

D_MODEL = 4096
BATCH = 4
SEQ = 2048
DEPTH = 4
DEC_BATCH = 32
PAST_LEN = 8192
GROUP_WIDTH = 1024
RET_HEADS = 8
RET_HEAD_DIM = 128
RET_CHUNK = 128
ATT_HEAD_DIM = 64
ATT_HEADS = 16
ATT_KV_HEADS = 2
ATT_GROUP = 8
WINDOW = 128
CONV_CH = 1024
CONV_WIDTH = 31
POOL_CH = 1024
POOL_WINDOWS = (2, 4, 8, 16)
POOL_GROUP = 256
POOL_MAX = 16
D_FF = 11008
FFN_CONV_WIDTH = 3
NORM_EPS = 1e-6
IN_WIDTH = 8448

COL_RQ, COL_RK, COL_RV, COL_RG = 0, 1024, 2048, 3072
COL_AQ, COL_AK, COL_AV = 4096, 5120, 5248
COL_CA, COL_CG, COL_PIN = 5376, 6400, 7424

NTOK = BATCH * SEQ
TILE = 128
TILES_PER_SEQ = SEQ // TILE
CONV_HALO = 32
POOL_HALO = 16
FFN_HALO = 8

BF16 = jnp.bfloat16
F32 = jnp.float32
VMEM_LIMIT = 58 * 1024 * 1024


def _dot(a, b):
    return jnp.dot(a, b, preferred_element_type=F32)


def _dot_nt(a, b):
    return lax.dot_general(a, b, (((1,), (1,)), ((), ())), preferred_element_type=F32)


def _dot_tn(a, b):
    return lax.dot_general(a, b, (((0,), (0,)), ((), ())), preferred_element_type=F32)


def _sigmoid(x):
    return 1.0 / (1.0 + jnp.exp(-x))


def _silu(x):
    return x * _sigmoid(x)


def _rms(x, g):
    return x * lax.rsqrt(jnp.mean(x * x, axis=-1, keepdims=True) + NORM_EPS) * g


def _center_norm(x):
    mu = jnp.mean(x, axis=-1, keepdims=True)
    xc = x - mu
    return xc * lax.rsqrt(jnp.mean(xc * xc, axis=-1, keepdims=True) + NORM_EPS)


def _params(sem):
    return pltpu.CompilerParams(dimension_semantics=sem, vmem_limit_bytes=VMEM_LIMIT)


def _rmsnorm_kernel(x_ref, g_ref, o_ref):
    o_ref[...] = _rms(x_ref[...], g_ref[...]).astype(o_ref.dtype)


def _rmsnorm(x, g, rows):
    m, d = x.shape
    return pl.pallas_call(
        _rmsnorm_kernel,
        grid=(m // rows,),
        in_specs=[pl.BlockSpec((rows, d), lambda i: (i, 0)),
                  pl.BlockSpec((1, d), lambda i: (0, 0))],
        out_specs=pl.BlockSpec((rows, d), lambda i: (i, 0)),
        out_shape=jax.ShapeDtypeStruct((m, d), BF16),
        compiler_params=_params(("arbitrary",)),
        name="rmsnorm",
    )(x, g.reshape(1, d))


def _mm_kernel(*refs, has_res):
    if has_res:
        xp_ref, xs_ref, w_ref, rp_ref, rs_ref, op_ref, os_ref, wbf_ref = refs
    else:
        xp_ref, xs_ref, w_ref, op_ref, os_ref, wbf_ref = refs

    @pl.when(pl.program_id(1) == 0)
    def _():
        wbf_ref[...] = w_ref[...].astype(BF16)
        acc = _dot(xs_ref[...], wbf_ref[...])
        if has_res:
            acc = acc + rs_ref[...]
        os_ref[...] = acc.astype(os_ref.dtype)

    acc = _dot(xp_ref[...], wbf_ref[...])
    if has_res:
        acc = acc + rp_ref[...]
    op_ref[...] = acc.astype(op_ref.dtype)


def _matmul(xp, xs, w, bm, bn, res=None, name="matmul"):
    m, k = xp.shape
    s = xs.shape[0]
    n = w.shape[1]
    in_specs = [pl.BlockSpec((bm, k), lambda j, i: (i, 0)),
                pl.BlockSpec((s, k), lambda j, i: (0, 0)),
                pl.BlockSpec((k, bn), lambda j, i: (0, j))]
    args = [xp, xs, w]
    if res is not None:
        in_specs += [pl.BlockSpec((bm, bn), lambda j, i: (i, j)),
                     pl.BlockSpec((s, bn), lambda j, i: (0, j))]
        args += list(res)
    return pl.pallas_call(
        functools.partial(_mm_kernel, has_res=res is not None),
        grid=(pl.cdiv(n, bn), m // bm),
        in_specs=in_specs,
        out_specs=[pl.BlockSpec((bm, bn), lambda j, i: (i, j)),
                   pl.BlockSpec((s, bn), lambda j, i: (0, j))],
        out_shape=[jax.ShapeDtypeStruct((m, n), F32), jax.ShapeDtypeStruct((s, n), F32)],
        scratch_shapes=[pltpu.VMEM((k, bn), BF16)],
        compiler_params=_params(("arbitrary", "arbitrary")),
        name=name,
    )(*args)


def _ffn_up_kernel(xp_ref, xs_ref, wg_ref, wv_ref, dw_ref, db_ref, past_ref,
                   actp_ref, acts_ref, ffnp_ref, ffns_ref,
                   wgbf_ref, wvbf_ref, gbuf_ref):
    i = pl.program_id(1)
    bm = xp_ref.shape[0]
    w0 = dw_ref[0:1, :]
    w1 = dw_ref[1:2, :]
    w2 = dw_ref[2:3, :]
    bias = db_ref[...]

    @pl.when(i == 0)
    def _():
        wgbf_ref[...] = wg_ref[...].astype(BF16)
        wvbf_ref[...] = wv_ref[...].astype(BF16)
        xs = xs_ref[...]
        g = _dot(xs, wgbf_ref[...])
        v = _dot(xs, wvbf_ref[...])
        c = w0 * past_ref[0] + w1 * past_ref[1] + w2 * g + bias
        acts_ref[...] = (_silu(c) * v).astype(acts_ref.dtype)
        ffns_ref[0] = past_ref[1]
        ffns_ref[1] = g

    @pl.when(i % (SEQ // bm) == 0)
    def _():
        gbuf_ref[0:FFN_HALO, :] = jnp.zeros((FFN_HALO, gbuf_ref.shape[1]), F32)

    xp = xp_ref[...]
    g = _dot(xp, wgbf_ref[...])
    v = _dot(xp, wvbf_ref[...])
    gbuf_ref[FFN_HALO:FFN_HALO + bm, :] = g
    g1 = gbuf_ref[FFN_HALO - 1:FFN_HALO - 1 + bm, :]
    g2 = gbuf_ref[FFN_HALO - 2:FFN_HALO - 2 + bm, :]
    c = w0 * g2 + w1 * g1 + w2 * g + bias
    actp_ref[...] = (_silu(c) * v).astype(actp_ref.dtype)
    ffnp_ref[...] = g[bm - 2:bm, :]
    gbuf_ref[0:FFN_HALO, :] = g[bm - FFN_HALO:bm, :]


def _ffn_up(xp, xs, w_up, dw_w, dw_b, past_t, bm, bn):
    m, k = xp.shape
    s = xs.shape[0]
    nb = D_FF // bn
    tiles_per_seq = SEQ // bm
    return pl.pallas_call(
        _ffn_up_kernel,
        grid=(nb, m // bm),
        in_specs=[pl.BlockSpec((bm, k), lambda j, i: (i, 0)),
                  pl.BlockSpec((s, k), lambda j, i: (0, 0)),
                  pl.BlockSpec((k, bn), lambda j, i: (0, j)),
                  pl.BlockSpec((k, bn), lambda j, i: (0, j + nb)),
                  pl.BlockSpec((FFN_CONV_WIDTH, bn), lambda j, i: (0, j)),
                  pl.BlockSpec((1, bn), lambda j, i: (0, j)),
                  pl.BlockSpec((2, s, bn), lambda j, i: (0, 0, j))],
        out_specs=[pl.BlockSpec((bm, bn), lambda j, i: (i, j)),
                   pl.BlockSpec((s, bn), lambda j, i: (0, j)),
                   pl.BlockSpec((None, 2, bn), lambda j, i: (i // tiles_per_seq, 0, j)),
                   pl.BlockSpec((2, s, bn), lambda j, i: (0, 0, j))],
        out_shape=[jax.ShapeDtypeStruct((m, D_FF), BF16),
                   jax.ShapeDtypeStruct((s, D_FF), BF16),
                   jax.ShapeDtypeStruct((BATCH, 2, D_FF), F32),
                   jax.ShapeDtypeStruct((2, s, D_FF), F32)],
        scratch_shapes=[pltpu.VMEM((k, bn), BF16), pltpu.VMEM((k, bn), BF16),
                        pltpu.VMEM((FFN_HALO + bm, bn), F32)],
        compiler_params=_params(("arbitrary", "arbitrary")),
        name="ffn_up",
    )(xp, xs, w_up, w_up, dw_w, dw_b.reshape(1, D_FF), past_t)


def _alibi_slope(head):
    return 2.0 ** (-8.0 * (head + 1) / ATT_HEADS)


def _mixer_prompt_kernel(h_ref, dmask_ref, qdec_ref, kdec_ref, cdec_ref, gn_ref, qg_ref, kg_ref, sink_ref,
                         cw_ref, cb_ref, lng_ref, lnb_ref, pw2_ref, poolw_ref, pscale_ref,
                         mix_ref, ret_ref, kwin_ref, vwin_ref, convp_ref, poolp_ref,
                         s_ref, kprev_ref, vprev_ref, uwin_ref, pwin_ref, pw2bf_ref, poolwbf_ref):
    b = pl.program_id(0)
    t = pl.program_id(1)

    @pl.when((b == 0) & (t == 0))
    def _():
        pw2bf_ref[...] = pw2_ref[...].astype(BF16)
        poolwbf_ref[...] = poolw_ref[...].astype(BF16)

    @pl.when(t == 0)
    def _():
        s_ref[...] = jnp.zeros(s_ref.shape, F32)
        kprev_ref[...] = jnp.zeros(kprev_ref.shape, F32)
        vprev_ref[...] = jnp.zeros(vprev_ref.shape, F32)
        uwin_ref[0:CONV_HALO, :] = jnp.zeros((CONV_HALO, CONV_CH), F32)
        pwin_ref[0:POOL_HALO, :] = jnp.zeros((POOL_HALO, POOL_CH), F32)

    for hh in range(RET_HEADS):
        lo = hh * RET_HEAD_DIM
        hi = lo + RET_HEAD_DIM
        q = h_ref[:, COL_RQ + lo:COL_RQ + hi]
        k = h_ref[:, COL_RK + lo:COL_RK + hi] * (RET_HEAD_DIM ** -0.5)
        v = h_ref[:, COL_RV + lo:COL_RV + hi]
        qb = q.astype(BF16)
        vb = v.astype(BF16)
        a = _dot_nt(qb, k.astype(BF16)) * dmask_ref[hh]
        s_old = s_ref[hh]
        o = _dot(a.astype(BF16), vb) + _dot(qb, s_old.astype(BF16)) * qdec_ref[hh]
        s_ref[hh] = s_old * cdec_ref[hh] + _dot_tn((k * kdec_ref[hh]).astype(BF16), vb)
        y = _center_norm(o) * gn_ref[:, lo:hi]
        gate = h_ref[:, COL_RG + lo:COL_RG + hi]
        mix_ref[:, lo:hi] = (_silu(gate) * y).astype(mix_ref.dtype)
    ret_ref[...] = s_ref[...]

    row = lax.broadcasted_iota(jnp.int32, (TILE, TILE), 0)
    col = lax.broadcasted_iota(jnp.int32, (TILE, TILE), 1)
    dist_cur = (row - col).astype(F32)
    dist_prev = dist_cur + float(TILE)
    valid_cur = row >= col
    valid_prev = (col >= row) & (t > 0)
    kc = h_ref[:, COL_AK:COL_AK + 128]
    vc = h_ref[:, COL_AV:COL_AV + 128]
    kn = jnp.concatenate(
        [_rms(kc[:, g * ATT_HEAD_DIM:(g + 1) * ATT_HEAD_DIM], kg_ref[...]) for g in range(ATT_KV_HEADS)], axis=-1)
    kwin_ref[...] = kn
    vwin_ref[...] = vc
    kp = kprev_ref[...]
    vp = vprev_ref[...]
    scale = ATT_HEAD_DIM ** -0.5
    for pair in range(ATT_HEADS // 2):
        outs = []
        for hh in (2 * pair, 2 * pair + 1):
            g = hh // ATT_GROUP
            sl = slice(g * ATT_HEAD_DIM, (g + 1) * ATT_HEAD_DIM)
            qh = _rms(h_ref[:, COL_AQ + hh * ATT_HEAD_DIM:COL_AQ + (hh + 1) * ATT_HEAD_DIM], qg_ref[...]).astype(BF16)
            slope = _alibi_slope(hh)
            s_cur = _dot_nt(qh, kn[:, sl].astype(BF16)) * scale - slope * dist_cur
            s_prev = _dot_nt(qh, kp[:, sl].astype(BF16)) * scale - slope * dist_prev
            s_cur = jnp.where(valid_cur, s_cur, -jnp.inf)
            s_prev = jnp.where(valid_prev, s_prev, -jnp.inf)
            sink = sink_ref[hh]
            mx = jnp.maximum(jnp.maximum(jnp.max(s_cur, axis=-1, keepdims=True),
                                         jnp.max(s_prev, axis=-1, keepdims=True)), sink)
            e_cur = jnp.exp(s_cur - mx)
            e_prev = jnp.exp(s_prev - mx)
            den = (jnp.sum(e_cur, axis=-1, keepdims=True) + jnp.sum(e_prev, axis=-1, keepdims=True)
                   + jnp.exp(sink - mx))
            inv = 1.0 / den
            o = (_dot((e_cur * inv).astype(BF16), vc[:, sl].astype(BF16))
                 + _dot((e_prev * inv).astype(BF16), vp[:, sl].astype(BF16)))
            outs.append(o)
        mix_ref[:, GROUP_WIDTH + pair * 128:GROUP_WIDTH + (pair + 1) * 128] = (
            jnp.concatenate(outs, axis=-1).astype(mix_ref.dtype))
    kprev_ref[...] = kn
    vprev_ref[...] = vc

    u = h_ref[:, COL_CA:COL_CA + CONV_CH] * _sigmoid(h_ref[:, COL_CG:COL_CG + CONV_CH])
    uwin_ref[CONV_HALO:CONV_HALO + TILE, :] = u
    base = CONV_HALO - (CONV_WIDTH - 1)
    half = TILE // 2
    cols = []
    for cc in range(CONV_CH // 128):
        cs = slice(cc * 128, (cc + 1) * 128)
        rows = []
        for rr in range(2):
            acc = jnp.zeros((half, 128), F32)
            for j in range(CONV_WIDTH):
                acc = acc + cw_ref[j:j + 1, cs] * uwin_ref[base + j + rr * half:base + j + (rr + 1) * half, cs]
            rows.append(acc)
        cols.append(jnp.concatenate(rows, axis=0))
    c = jnp.concatenate(cols, axis=-1) + cb_ref[...]
    c = _silu(_center_norm(c) * lng_ref[...] + lnb_ref[...])
    mix_ref[:, 2 * GROUP_WIDTH:3 * GROUP_WIDTH] = _dot(c.astype(BF16), pw2bf_ref[...]).astype(mix_ref.dtype)
    convp_ref[...] = uwin_ref[CONV_HALO + TILE - (CONV_WIDTH - 1):CONV_HALO + TILE, :]
    uwin_ref[0:CONV_HALO, :] = uwin_ref[TILE:TILE + CONV_HALO, :]

    pin = h_ref[:, COL_PIN:COL_PIN + POOL_CH]
    pwin_ref[POOL_HALO:POOL_HALO + TILE, :] = pin
    pos = (t * TILE + lax.broadcasted_iota(jnp.int32, (TILE, 1), 0) + 1).astype(F32)
    for gi, w in enumerate(POOL_WINDOWS):
        cs = slice(gi * POOL_GROUP, (gi + 1) * POOL_GROUP)
        xg = pin[:, cs]
        sm = xg
        for d in range(1, w):
            sm = sm + pwin_ref[POOL_HALO - d:POOL_HALO - d + TILE, cs]
        cnt = jnp.minimum(float(w), pos)
        dd = sm / cnt - xg
        y = _dot(dd.astype(BF16), poolwbf_ref[gi]) * pscale_ref[:, cs]
        mix_ref[:, 3 * GROUP_WIDTH + gi * POOL_GROUP:3 * GROUP_WIDTH + (gi + 1) * POOL_GROUP] = y.astype(mix_ref.dtype)
    poolp_ref[...] = pwin_ref[POOL_HALO + TILE - (POOL_MAX - 1):POOL_HALO + TILE, :]
    pwin_ref[0:POOL_HALO, :] = pwin_ref[TILE:TILE + POOL_HALO, :]


def _full(shape):
    nd = len(shape)
    return pl.BlockSpec(shape, lambda b, t: (0,) * nd)


def _mixer_prompt(h, tables, gn, qg, kg, sinks, cw, cb, lng, lnb, pw2, poolw, pscale):
    dmask, qdec, kdec, cdec = tables
    smem = pl.BlockSpec(memory_space=pltpu.SMEM)
    in_specs = [pl.BlockSpec((TILE, IN_WIDTH), lambda b, t: (b * TILES_PER_SEQ + t, 0)),
                _full(dmask.shape), _full(qdec.shape), _full(kdec.shape), _full(cdec.shape),
                _full((1, GROUP_WIDTH)), _full((1, ATT_HEAD_DIM)), _full((1, ATT_HEAD_DIM)), smem,
                _full((CONV_WIDTH, CONV_CH)), _full((1, CONV_CH)), _full((1, CONV_CH)), _full((1, CONV_CH)),
                _full((CONV_CH, GROUP_WIDTH)), _full((4, POOL_GROUP, POOL_GROUP)), _full((1, POOL_CH))]
    out_specs = [pl.BlockSpec((TILE, 4 * GROUP_WIDTH), lambda b, t: (b * TILES_PER_SEQ + t, 0)),
                 pl.BlockSpec((None, RET_HEADS, RET_HEAD_DIM, RET_HEAD_DIM), lambda b, t: (b, 0, 0, 0)),
                 pl.BlockSpec((None, WINDOW, 128), lambda b, t: (b, 0, 0)),
                 pl.BlockSpec((None, WINDOW, 128), lambda b, t: (b, 0, 0)),
                 pl.BlockSpec((None, CONV_WIDTH - 1, CONV_CH), lambda b, t: (b, 0, 0)),
                 pl.BlockSpec((None, POOL_MAX - 1, POOL_CH), lambda b, t: (b, 0, 0))]
    out_shape = [jax.ShapeDtypeStruct((NTOK, 4 * GROUP_WIDTH), BF16),
                 jax.ShapeDtypeStruct((BATCH, RET_HEADS, RET_HEAD_DIM, RET_HEAD_DIM), F32),
                 jax.ShapeDtypeStruct((BATCH, WINDOW, 128), F32),
                 jax.ShapeDtypeStruct((BATCH, WINDOW, 128), F32),
                 jax.ShapeDtypeStruct((BATCH, CONV_WIDTH - 1, CONV_CH), F32),
                 jax.ShapeDtypeStruct((BATCH, POOL_MAX - 1, POOL_CH), F32)]
    scratch = [pltpu.VMEM((RET_HEADS, RET_HEAD_DIM, RET_HEAD_DIM), F32),
               pltpu.VMEM((TILE, 128), F32), pltpu.VMEM((TILE, 128), F32),
               pltpu.VMEM((CONV_HALO + TILE, CONV_CH), F32),
               pltpu.VMEM((POOL_HALO + TILE, POOL_CH), F32),
               pltpu.VMEM((CONV_CH, GROUP_WIDTH), BF16),
               pltpu.VMEM((4, POOL_GROUP, POOL_GROUP), BF16)]
    return pl.pallas_call(
        _mixer_prompt_kernel,
        grid=(BATCH, TILES_PER_SEQ),
        in_specs=in_specs, out_specs=out_specs, out_shape=out_shape, scratch_shapes=scratch,
        compiler_params=_params(("arbitrary", "arbitrary")),
        name="mixer_prompt",
    )(h, dmask, qdec, kdec, cdec, gn.reshape(1, -1), qg.reshape(1, -1), kg.reshape(1, -1), sinks,
      cw, cb.reshape(1, -1), lng.reshape(1, -1), lnb.reshape(1, -1), pw2, poolw, pscale.reshape(1, -1))


def _mixer_sample_kernel(row_ref, hs_ref, s0_ref, ck_ref, cv_ref, cst_ref, pst_ref, gdec_ref, slope_ref,
                         gn_ref, qg_ref, kg_ref, sink_ref,
                         cw_ref, cb_ref, lng_ref, lnb_ref, pw2_ref, poolw_ref, pscale_ref,
                         ra_ref, mix_ref, ret_ref, kwin_ref, vwin_ref, convs_ref, pools_ref):
    b = pl.program_id(0)
    nb = pl.num_programs(0)

    for hh in range(RET_HEADS):
        lo = hh * RET_HEAD_DIM
        hi = lo + RET_HEAD_DIM
        q = row_ref[:, COL_RQ + lo:COL_RQ + hi]
        k = row_ref[:, COL_RK + lo:COL_RK + hi] * (RET_HEAD_DIM ** -0.5)
        v = row_ref[:, COL_RV + lo:COL_RV + hi]
        gate = row_ref[:, COL_RG + lo:COL_RG + hi]
        gdec = gdec_ref[hh]
        s_old = s0_ref[hh]
        q8 = jnp.broadcast_to(q, (8, RET_HEAD_DIM)).astype(BF16)
        qs = _dot(q8, s_old.astype(BF16))[0:1, :]
        a = jnp.sum(q * k, axis=-1, keepdims=True)
        o = a * v + qs * gdec
        k8 = jnp.where(lax.broadcasted_iota(jnp.int32, (8, RET_HEAD_DIM), 0) == 0, k, 0.0).astype(BF16)
        v8 = jnp.broadcast_to(v, (8, RET_HEAD_DIM)).astype(BF16)
        ret_ref[hh] = s_old * gdec + _dot_tn(k8, v8)
        y = _center_norm(o) * gn_ref[:, lo:hi]
        ra_ref[:, lo:hi] = _silu(gate) * y

    knew = row_ref[:, COL_AK:COL_AK + 128]
    vnew = row_ref[:, COL_AV:COL_AV + 128]
    scale = ATT_HEAD_DIM ** -0.5
    dist = (WINDOW - lax.broadcasted_iota(jnp.int32, (1, WINDOW), 1)).astype(F32)
    kn_parts = []
    for g in range(ATT_KV_HEADS):
        sl = slice(g * ATT_HEAD_DIM, (g + 1) * ATT_HEAD_DIM)
        kng = _rms(knew[:, sl], kg_ref[...])
        kn_parts.append(kng)
        vng = vnew[:, sl]
        q0 = COL_AQ + g * ATT_GROUP * ATT_HEAD_DIM
        qs8 = jnp.concatenate(
            [row_ref[:, q0 + i * ATT_HEAD_DIM:q0 + (i + 1) * ATT_HEAD_DIM] for i in range(ATT_GROUP)], axis=0)
        qn = _rms(qs8, qg_ref[...])
        slope = slope_ref[g]
        sink = sink_ref[g]
        s_past = _dot_nt(qn.astype(BF16), ck_ref[:, sl].astype(BF16)) * scale - slope * dist
        s_new = jnp.sum(qn * kng, axis=-1, keepdims=True) * scale
        mx = jnp.maximum(jnp.maximum(jnp.max(s_past, axis=-1, keepdims=True), s_new), sink)
        e_past = jnp.exp(s_past - mx)
        e_new = jnp.exp(s_new - mx)
        den = jnp.sum(e_past, axis=-1, keepdims=True) + e_new + jnp.exp(sink - mx)
        inv = 1.0 / den
        o = _dot((e_past * inv).astype(BF16), cv_ref[:, sl].astype(BF16)) + (e_new * inv) * vng
        for i in range(ATT_GROUP):
            c0 = GROUP_WIDTH + (g * ATT_GROUP + i) * ATT_HEAD_DIM
            ra_ref[:, c0:c0 + ATT_HEAD_DIM] = o[i:i + 1, :]
    kwin_ref[0:WINDOW - 1, :] = ck_ref[1:WINDOW, :]
    kwin_ref[WINDOW - 1:WINDOW, :] = jnp.concatenate(kn_parts, axis=-1)
    vwin_ref[0:WINDOW - 1, :] = cv_ref[1:WINDOW, :]
    vwin_ref[WINDOW - 1:WINDOW, :] = vnew

    @pl.when(b == nb - 1)
    def _():
        u = hs_ref[:, COL_CA:COL_CA + CONV_CH] * _sigmoid(hs_ref[:, COL_CG:COL_CG + CONV_CH])
        c = cw_ref[CONV_WIDTH - 1:CONV_WIDTH, :] * u
        for j in range(CONV_WIDTH - 1):
            c = c + cw_ref[j:j + 1, :] * cst_ref[j]
        c = c + cb_ref[...]
        c = _silu(_center_norm(c) * lng_ref[...] + lnb_ref[...])
        mix_ref[:, 0:GROUP_WIDTH] = _dot(c.astype(BF16), pw2_ref[...].astype(BF16)).astype(mix_ref.dtype)
        for j in range(CONV_WIDTH - 2):
            convs_ref[j] = cst_ref[j + 1]
        convs_ref[CONV_WIDTH - 2] = u

        pin = hs_ref[:, COL_PIN:COL_PIN + POOL_CH]
        npast = POOL_MAX - 1
        for gi, w in enumerate(POOL_WINDOWS):
            cs = slice(gi * POOL_GROUP, (gi + 1) * POOL_GROUP)
            xg = pin[:, cs]
            sm = xg
            for d in range(1, w):
                sm = sm + pst_ref[npast - d, :, cs]
            dd = sm / float(w) - xg
            y = _dot(dd.astype(BF16), poolw_ref[gi].astype(BF16)) * pscale_ref[:, cs]
            mix_ref[:, GROUP_WIDTH + gi * POOL_GROUP:GROUP_WIDTH + (gi + 1) * POOL_GROUP] = y.astype(mix_ref.dtype)
        for j in range(npast - 1):
            pools_ref[j] = pst_ref[j + 1]
        pools_ref[npast - 1] = pin


def _mixer_sample(layer, hs, state_ret, cache_k, cache_v, cst, pst, gdec, slopes,
                  gn, qg, kg, sinks, cw, cb, lng, lnb, pw2, poolw, pscale):
    nb = DEC_BATCH

    def full(shape):
        nd = len(shape)
        return pl.BlockSpec(shape, lambda b: (0,) * nd)

    in_specs = [pl.BlockSpec((None, 1, IN_WIDTH), lambda b: (b, 0, 0)),
                full((nb, IN_WIDTH)),
                pl.BlockSpec((None, None, RET_HEADS, RET_HEAD_DIM, RET_HEAD_DIM), lambda b: (layer, b, 0, 0, 0)),
                pl.BlockSpec((None, None, WINDOW, 128), lambda b: (layer, b, 0, 0)),
                pl.BlockSpec((None, None, WINDOW, 128), lambda b: (layer, b, 0, 0)),
                full((CONV_WIDTH - 1, nb, CONV_CH)), full((POOL_MAX - 1, nb, POOL_CH)),
                full((RET_HEADS, 1, RET_HEAD_DIM)), full((ATT_KV_HEADS, ATT_GROUP, 1)),
                full((1, GROUP_WIDTH)), full((1, ATT_HEAD_DIM)), full((1, ATT_HEAD_DIM)),
                full((ATT_KV_HEADS, ATT_GROUP, 1)),
                full((CONV_WIDTH, CONV_CH)), full((1, CONV_CH)), full((1, CONV_CH)), full((1, CONV_CH)),
                full((CONV_CH, GROUP_WIDTH)), full((4, POOL_GROUP, POOL_GROUP)), full((1, POOL_CH))]
    out_specs = [pl.BlockSpec((None, 1, 2 * GROUP_WIDTH), lambda b: (b, 0, 0)),
                 full((nb, 2 * GROUP_WIDTH)),
                 pl.BlockSpec((None, RET_HEADS, RET_HEAD_DIM, RET_HEAD_DIM), lambda b: (b, 0, 0, 0)),
                 pl.BlockSpec((None, WINDOW, 128), lambda b: (b, 0, 0)),
                 pl.BlockSpec((None, WINDOW, 128), lambda b: (b, 0, 0)),
                 full((CONV_WIDTH - 1, nb, CONV_CH)), full((POOL_MAX - 1, nb, POOL_CH))]
    out_shape = [jax.ShapeDtypeStruct((nb, 1, 2 * GROUP_WIDTH), F32),
                 jax.ShapeDtypeStruct((nb, 2 * GROUP_WIDTH), BF16),
                 jax.ShapeDtypeStruct((nb, RET_HEADS, RET_HEAD_DIM, RET_HEAD_DIM), F32),
                 jax.ShapeDtypeStruct((nb, WINDOW, 128), F32),
                 jax.ShapeDtypeStruct((nb, WINDOW, 128), F32),
                 jax.ShapeDtypeStruct((CONV_WIDTH - 1, nb, CONV_CH), F32),
                 jax.ShapeDtypeStruct((POOL_MAX - 1, nb, POOL_CH), F32)]
    ra, cp, *states = pl.pallas_call(
        _mixer_sample_kernel,
        grid=(nb,),
        in_specs=in_specs, out_specs=out_specs, out_shape=out_shape,
        compiler_params=_params(("arbitrary",)),
        name="mixer_sample",
    )(hs.reshape(nb, 1, IN_WIDTH), hs, state_ret, cache_k, cache_v, cst, pst, gdec, slopes,
      gn.reshape(1, -1), qg.reshape(1, -1), kg.reshape(1, -1), sinks.reshape(ATT_KV_HEADS, ATT_GROUP, 1),
      cw, cb.reshape(1, -1), lng.reshape(1, -1), lnb.reshape(1, -1), pw2, poolw, pscale.reshape(1, -1))
    mix = jnp.concatenate([ra.reshape(nb, 2 * GROUP_WIDTH).astype(BF16), cp], axis=-1)
    return (mix, *states)


def _retention_tables():
    c = RET_CHUNK
    log_g = jnp.log1p(-jnp.exp2(-5.0 - jnp.arange(RET_HEADS, dtype=F32)))
    idx = jnp.arange(c, dtype=F32)
    rel = idx[:, None] - idx[None, :]
    dmask = jnp.where(rel >= 0, jnp.exp(log_g[:, None, None] * jnp.maximum(rel, 0.0)), 0.0)
    ones = jnp.ones((1, 1, RET_HEAD_DIM), F32)
    qdec = jnp.exp(log_g[:, None] * (idx + 1.0))[:, :, None] * ones
    kdec = jnp.exp(log_g[:, None] * (c - 1.0 - idx))[:, :, None] * ones
    cdec = jnp.exp(log_g * c)[:, None, None] * jnp.ones((1, RET_HEAD_DIM, RET_HEAD_DIM), F32)
    gdec = jnp.exp(log_g)[:, None, None] * ones
    return (dmask, qdec, kdec, cdec), gdec


def kernel(x_prompt, x_sample, state_ret, cache_k_win, cache_v_win, state_conv, state_pool, state_ffn,
           norm1_g, w_in, ret_gn_g, q_norm_g, k_norm_g, att_sinks, conv_dw_w, conv_dw_b, conv_ln_g, conv_ln_b,
           conv_pw2, pool_w, pool_scale, w_out, norm2_g, ffn_up, ffn_dw_w, ffn_dw_b, ffn_down):
    tables, gdec = _retention_tables()
    slopes = jnp.exp2(-8.0 * jnp.arange(1, ATT_HEADS + 1, dtype=F32) / ATT_HEADS).reshape(ATT_KV_HEADS, ATT_GROUP, 1)
    cache_k = cache_k_win.reshape(DEPTH, DEC_BATCH, WINDOW, 128)
    cache_v = cache_v_win.reshape(DEPTH, DEC_BATCH, WINDOW, 128)

    xp = x_prompt.reshape(NTOK, D_MODEL)
    xs = x_sample.reshape(DEC_BATCH, D_MODEL)
    outs_p = [[] for _ in range(6)]
    outs_s = [[] for _ in range(6)]
    for l in range(DEPTH):
        xnp = _rmsnorm(xp, norm1_g[l], 256)
        xns = _rmsnorm(xs, norm1_g[l], DEC_BATCH)
        hp, hs = _matmul(xnp, xns, w_in[l], 512, 768, name="w_in")
        mixp, ret_p, kwin_p, vwin_p, conv_p, pool_p = _mixer_prompt(
            hp, tables, ret_gn_g[l], q_norm_g[l], k_norm_g[l], att_sinks[l], conv_dw_w[l], conv_dw_b[l],
            conv_ln_g[l], conv_ln_b[l], conv_pw2[l], pool_w[l], pool_scale[l])
        cst = jnp.swapaxes(state_conv[l], 0, 1)
        pst = jnp.swapaxes(state_pool[l], 0, 1)
        mixs, ret_s, kwin_s, vwin_s, conv_s, pool_s = _mixer_sample(
            l, hs, state_ret, cache_k, cache_v, cst, pst, gdec, slopes,
            ret_gn_g[l], q_norm_g[l], k_norm_g[l], att_sinks[l], conv_dw_w[l], conv_dw_b[l],
            conv_ln_g[l], conv_ln_b[l], conv_pw2[l], pool_w[l], pool_scale[l])
        xp, xs = _matmul(mixp, mixs, w_out[l], 1024, 512, res=(xp, xs), name="w_out")
        xnp = _rmsnorm(xp, norm2_g[l], 256)
        xns = _rmsnorm(xs, norm2_g[l], DEC_BATCH)
        fst = jnp.swapaxes(state_ffn[l], 0, 1)
        actp, acts, ffn_p, ffn_s = _ffn_up(xnp, xns, ffn_up[l], ffn_dw_w[l], ffn_dw_b[l], fst, 1024, 256)
        xp, xs = _matmul(actp, acts, ffn_down[l], 512, 256, res=(xp, xs), name="ffn_down")
        for lst, v in zip(outs_p, (ret_p, kwin_p.reshape(BATCH, WINDOW, ATT_KV_HEADS, ATT_HEAD_DIM),
                                   vwin_p.reshape(BATCH, WINDOW, ATT_KV_HEADS, ATT_HEAD_DIM), conv_p, pool_p, ffn_p)):
            lst.append(v)
        for lst, v in zip(outs_s, (ret_s, kwin_s.reshape(DEC_BATCH, WINDOW, ATT_KV_HEADS, ATT_HEAD_DIM),
                                   vwin_s.reshape(DEC_BATCH, WINDOW, ATT_KV_HEADS, ATT_HEAD_DIM),
                                   jnp.swapaxes(conv_s, 0, 1), jnp.swapaxes(pool_s, 0, 1),
                                   jnp.swapaxes(ffn_s, 0, 1))):
            lst.append(v)
    y_prompt = xp.reshape(BATCH, SEQ, D_MODEL)
    y_sample = xs.reshape(DEC_BATCH, 1, D_MODEL)
    return (y_prompt, y_sample, *[jnp.stack(v) for v in outs_p], *[jnp.stack(v) for v in outs_s])
```

```python
import functools

import jax
import jax.numpy as jnp
from jax import lax
from jax.experimental import pallas as pl
from jax.experimental.pallas import tpu as pltpu

D_MODEL = 4096
BATCH = 4
SEQ = 2048
DEPTH = 4
DEC_BATCH = 32
PAST_LEN = 8192
GROUP_WIDTH = 1024
RET_HEADS = 8
RET_HEAD_DIM = 128
RET_CHUNK = 128
ATT_HEAD_DIM = 64
ATT_HEADS = 16
ATT_KV_HEADS = 2
ATT_GROUP = 8
WINDOW = 128
CONV_CH = 1024
CONV_WIDTH = 31
POOL_CH = 1024
POOL_WINDOWS = (2, 4, 8, 16)
POOL_GROUP = 256
POOL_MAX = 16
D_FF = 11008
FFN_CONV_WIDTH = 3
NORM_EPS = 1e-6
IN_WIDTH = 8448

COL_RQ, COL_RK, COL_RV, COL_RG = 0, 1024, 2048, 3072
COL_AQ, COL_AK, COL_AV = 4096, 5120, 5248
COL_CA, COL_CG, COL_PIN = 5376, 6400, 7424

NTOK = BATCH * SEQ
TILE = 128
TILES_PER_SEQ = SEQ // TILE
CONV_HALO = 32
POOL_HALO = 16
FFN_HALO = 8
FFN_SUBBLOCKS = 1

BF16 = jnp.bfloat16
F32 = jnp.float32
VMEM_LIMIT = 58 * 1024 * 1024


def _dot(a, b):
    return jnp.dot(a, b, preferred_element_type=F32)


def _dot_nt(a, b):
    return lax.dot_general(a, b, (((1,), (1,)), ((), ())), preferred_element_type=F32)


def _dot_tn(a, b):
    return lax.dot_general(a, b, (((0,), (0,)), ((), ())), preferred_element_type=F32)


def _sigmoid(x):
    return 1.0 / (1.0 + jnp.exp(-x))


def _silu(x):
    return x * _sigmoid(x)


def _rms(x, g):
    return x * lax.rsqrt(jnp.mean(x * x, axis=-1, keepdims=True) + NORM_EPS) * g


def _center_norm(x):
    mu = jnp.mean(x, axis=-1, keepdims=True)
    xc = x - mu
    return xc * lax.rsqrt(jnp.mean(xc * xc, axis=-1, keepdims=True) + NORM_EPS)


def _params(sem):
    return pltpu.CompilerParams(dimension_semantics=sem, vmem_limit_bytes=VMEM_LIMIT)


def _rmsnorm_kernel(x_ref, g_ref, o_ref):
    o_ref[...] = _rms(x_ref[...], g_ref[...]).astype(o_ref.dtype)


def _rmsnorm(x, g, rows):
    m, d = x.shape
    return pl.pallas_call(
        _rmsnorm_kernel,
        grid=(m // rows,),
        in_specs=[pl.BlockSpec((rows, d), lambda i: (i, 0)),
                  pl.BlockSpec((1, d), lambda i: (0, 0))],
        out_specs=pl.BlockSpec((rows, d), lambda i: (i, 0)),
        out_shape=jax.ShapeDtypeStruct((m, d), BF16),
        compiler_params=_params(("arbitrary",)),
        name="rmsnorm",
    )(x, g.reshape(1, d))


def _mm_kernel(*refs, has_res):
    if has_res:
        xp_ref, xs_ref, w_ref, rp_ref, rs_ref, op_ref, os_ref, wbf_ref = refs
    else:
        xp_ref, xs_ref, w_ref, op_ref, os_ref, wbf_ref = refs

    @pl.when(pl.program_id(1) == 0)
    def _():
        wbf_ref[...] = w_ref[...].astype(BF16)
        acc = _dot(xs_ref[...], wbf_ref[...])
        if has_res:
            acc = acc + rs_ref[...]
        os_ref[...] = acc.astype(os_ref.dtype)

    acc = _dot(xp_ref[...], wbf_ref[...])
    if has_res:
        acc = acc + rp_ref[...]
    op_ref[...] = acc.astype(op_ref.dtype)


def _matmul(xp, xs, w, layer, bm, bn, res=None, name="matmul"):
    m, k = xp.shape
    s = xs.shape[0]
    n = w.shape[2]
    in_specs = [pl.BlockSpec((bm, k), lambda j, i: (i, 0)),
                pl.BlockSpec((s, k), lambda j, i: (0, 0)),
                pl.BlockSpec((None, k, bn), lambda j, i: (layer, 0, j))]
    args = [xp, xs, w]
    if res is not None:
        in_specs += [pl.BlockSpec((bm, bn), lambda j, i: (i, j)),
                     pl.BlockSpec((s, bn), lambda j, i: (0, j))]
        args += list(res)
    return pl.pallas_call(
        functools.partial(_mm_kernel, has_res=res is not None),
        grid=(pl.cdiv(n, bn), m // bm),
        in_specs=in_specs,
        out_specs=[pl.BlockSpec((bm, bn), lambda j, i: (i, j)),
                   pl.BlockSpec((s, bn), lambda j, i: (0, j))],
        out_shape=[jax.ShapeDtypeStruct((m, n), F32), jax.ShapeDtypeStruct((s, n), F32)],
        scratch_shapes=[pltpu.VMEM((k, bn), BF16)],
        compiler_params=_params(("arbitrary", "arbitrary")),
        name=name,
    )(*args)


def _ffn_up_kernel(xp_ref, xs_ref, wg_ref, wv_ref, dw_ref, db_ref, past_ref,
                   actp_ref, acts_ref, ffnp_ref, ffns_ref,
                   wgbf_ref, wvbf_ref, gbuf_ref):
    i = pl.program_id(1)
    bm = xp_ref.shape[0]
    w0 = dw_ref[0:1, :]
    w1 = dw_ref[1:2, :]
    w2 = dw_ref[2:3, :]
    bias = db_ref[...]

    @pl.when(i == 0)
    def _():
        wgbf_ref[...] = wg_ref[...].astype(BF16)
        wvbf_ref[...] = wv_ref[...].astype(BF16)
        xs = xs_ref[...]
        g = _dot(xs, wgbf_ref[...])
        v = _dot(xs, wvbf_ref[...])
        c = w0 * past_ref[0] + w1 * past_ref[1] + w2 * g + bias
        acts_ref[...] = (_silu(c) * v).astype(acts_ref.dtype)
        ffns_ref[0] = past_ref[1]
        ffns_ref[1] = g

    @pl.when(i % (SEQ // bm) == 0)
    def _():
        gbuf_ref[0:FFN_HALO, :] = jnp.zeros((FFN_HALO, gbuf_ref.shape[1]), F32)

    sub = bm // FFN_SUBBLOCKS
    for r in range(FFN_SUBBLOCKS):
        xp = xp_ref[r * sub:(r + 1) * sub, :]
        g = _dot(xp, wgbf_ref[...])
        v = _dot(xp, wvbf_ref[...])
        lo = FFN_HALO + r * sub
        gbuf_ref[lo:lo + sub, :] = g
        g1 = gbuf_ref[lo - 1:lo - 1 + sub, :]
        g2 = gbuf_ref[lo - 2:lo - 2 + sub, :]
        c = w0 * g2 + w1 * g1 + w2 * g + bias
        actp_ref[r * sub:(r + 1) * sub, :] = (_silu(c) * v).astype(actp_ref.dtype)
    ffnp_ref[...] = gbuf_ref[FFN_HALO + bm - 2:FFN_HALO + bm, :]
    gbuf_ref[0:FFN_HALO, :] = gbuf_ref[bm:bm + FFN_HALO, :]


def _ffn_up(xp, xs, w_up, layer, dw_w, dw_b, past_t, bm, bn):
    m, k = xp.shape
    s = xs.shape[0]
    nb = D_FF // bn
    tiles_per_seq = SEQ // bm
    return pl.pallas_call(
        _ffn_up_kernel,
        grid=(nb, m // bm),
        in_specs=[pl.BlockSpec((bm, k), lambda j, i: (i, 0)),
                  pl.BlockSpec((s, k), lambda j, i: (0, 0)),
                  pl.BlockSpec((None, k, bn), lambda j, i: (layer, 0, j)),
                  pl.BlockSpec((None, k, bn), lambda j, i: (layer, 0, j + nb)),
                  pl.BlockSpec((FFN_CONV_WIDTH, bn), lambda j, i: (0, j)),
                  pl.BlockSpec((1, bn), lambda j, i: (0, j)),
                  pl.BlockSpec((2, s, bn), lambda j, i: (0, 0, j))],
        out_specs=[pl.BlockSpec((bm, bn), lambda j, i: (i, j)),
                   pl.BlockSpec((s, bn), lambda j, i: (0, j)),
                   pl.BlockSpec((None, 2, bn), lambda j, i: (i // tiles_per_seq, 0, j)),
                   pl.BlockSpec((2, s, bn), lambda j, i: (0, 0, j))],
        out_shape=[jax.ShapeDtypeStruct((m, D_FF), BF16),
                   jax.ShapeDtypeStruct((s, D_FF), BF16),
                   jax.ShapeDtypeStruct((BATCH, 2, D_FF), F32),
                   jax.ShapeDtypeStruct((2, s, D_FF), F32)],
        scratch_shapes=[pltpu.VMEM((k, bn), BF16), pltpu.VMEM((k, bn), BF16),
                        pltpu.VMEM((FFN_HALO + bm, bn), F32)],
        compiler_params=_params(("arbitrary", "arbitrary")),
        name="ffn_up",
    )(xp, xs, w_up, w_up, dw_w, dw_b.reshape(1, D_FF), past_t)


def _mixer_prompt_kernel(h_ref, dmask_ref, qdec_ref, kdec_ref, cdec_ref, gn_ref, qg_ref, kg_ref,
                         sinkb_ref, biasc_ref, biasp_ref,
                         cw_ref, cb_ref, lng_ref, lnb_ref, pw2_ref, poolw_ref, pscale_ref,
                         mix_ref, ret_ref, kwin_ref, vwin_ref, convp_ref, poolp_ref,
                         s_ref, kprev_ref, vprev_ref, uwin_ref, pwin_ref, pw2bf_ref, poolwbf_ref):
    b = pl.program_id(0)
    t = pl.program_id(1)

    @pl.when((b == 0) & (t == 0))
    def _():
        pw2bf_ref[...] = pw2_ref[...].astype(BF16)
        poolwbf_ref[...] = poolw_ref[...].astype(BF16)

    @pl.when(t == 0)
    def _():
        s_ref[...] = jnp.zeros(s_ref.shape, F32)
        kprev_ref[...] = jnp.zeros(kprev_ref.shape, F32)
        vprev_ref[...] = jnp.zeros(vprev_ref.shape, F32)
        uwin_ref[0:CONV_HALO, :] = jnp.zeros((CONV_HALO, CONV_CH), F32)
        pwin_ref[0:POOL_HALO, :] = jnp.zeros((POOL_HALO, POOL_CH), F32)

    ret_out, ret_state = [], []
    for hh in range(RET_HEADS):
        lo = hh * RET_HEAD_DIM
        hi = lo + RET_HEAD_DIM
        q = h_ref[:, COL_RQ + lo:COL_RQ + hi]
        k = h_ref[:, COL_RK + lo:COL_RK + hi] * (RET_HEAD_DIM ** -0.5)
        v = h_ref[:, COL_RV + lo:COL_RV + hi]
        qb = q.astype(BF16)
        vb = v.astype(BF16)
        a = _dot_nt(qb, k.astype(BF16)) * dmask_ref[hh]
        s_old = s_ref[hh]
        o = _dot(a.astype(BF16), vb) + _dot(qb, s_old.astype(BF16)) * qdec_ref[hh]
        ret_state.append(s_old * cdec_ref[hh] + _dot_tn((k * kdec_ref[hh]).astype(BF16), vb))
        y = _center_norm(o) * gn_ref[:, lo:hi]
        gate = h_ref[:, COL_RG + lo:COL_RG + hi]
        ret_out.append((_silu(gate) * y).astype(mix_ref.dtype))
    mix_ref[:, 0:GROUP_WIDTH] = jnp.concatenate(ret_out, axis=-1)
    for hh in range(RET_HEADS):
        s_ref[hh] = ret_state[hh]
        ret_ref[hh] = ret_state[hh]

    lane = lax.broadcasted_iota(jnp.int32, (1, 128), 1)
    half_mask = (lane < ATT_HEAD_DIM, lane >= ATT_HEAD_DIM)

    def pair_rms(x, gain):
        sq = x * x
        ms = [jnp.sum(jnp.where(hm, sq, 0.0), axis=-1, keepdims=True) * (1.0 / ATT_HEAD_DIM) for hm in half_mask]
        r = jnp.where(half_mask[0], lax.rsqrt(ms[0] + NORM_EPS), lax.rsqrt(ms[1] + NORM_EPS))
        return x * r * gain

    kn = pair_rms(h_ref[:, COL_AK:COL_AK + 128], kg_ref[...])
    vc = h_ref[:, COL_AV:COL_AV + 128]
    kwin_ref[...] = kn
    vwin_ref[...] = vc
    kp = kprev_ref[...]
    vp = vprev_ref[...]
    kprev_ref[...] = kn
    vprev_ref[...] = vc
    swapped = {id(x): pltpu.roll(x, ATT_HEAD_DIM, axis=1) for x in (kn, vc, kp, vp)}

    def half_of(x, src, dst):
        y = x if src == dst else swapped[id(x)]
        return jnp.where(half_mask[dst], y, 0.0).astype(BF16)

    scale = ATT_HEAD_DIM ** -0.5
    att_out = []
    for g in range(ATT_KV_HEADS):
        q0 = COL_AQ + g * ATT_GROUP * ATT_HEAD_DIM
        qst = jnp.concatenate([h_ref[:, q0 + p * 128:q0 + (p + 1) * 128] for p in range(ATT_GROUP // 2)], axis=0)
        qn = pair_rms(qst, qg_ref[...]).astype(BF16)
        o = None
        for par in range(2):
            s_cur = _dot_nt(qn, half_of(kn, g, par)) * scale + biasc_ref[g, par]
            s_prev = _dot_nt(qn, half_of(kp, g, par)) * scale + biasp_ref[g, par]
            s_prev = jnp.where(t > 0, s_prev, -jnp.inf)
            sink = sinkb_ref[g, par]
            mx = jnp.max(jnp.maximum(jnp.maximum(s_cur, s_prev), sink), axis=-1, keepdims=True)
            e_cur = jnp.exp(s_cur - mx)
            e_prev = jnp.exp(s_prev - mx)
            den = jnp.sum(e_cur + e_prev, axis=-1, keepdims=True) + jnp.exp(sink[:, 0:1] - mx)
            inv = 1.0 / den
            od = (_dot((e_cur * inv).astype(BF16), half_of(vc, g, par))
                  + _dot((e_prev * inv).astype(BF16), half_of(vp, g, par)))
            o = od if o is None else o + od
        att_out += [o[p * 128:(p + 1) * 128, :].astype(mix_ref.dtype) for p in range(ATT_GROUP // 2)]
    mix_ref[:, GROUP_WIDTH:2 * GROUP_WIDTH] = jnp.concatenate(att_out, axis=-1)

    u = h_ref[:, COL_CA:COL_CA + CONV_CH] * _sigmoid(h_ref[:, COL_CG:COL_CG + CONV_CH])
    uwin_ref[CONV_HALO:CONV_HALO + TILE, :] = u
    wrows = TILE + 8
    cols = []
    for cc in range(CONV_CH // 128):
        cs = slice(cc * 128, (cc + 1) * 128)
        acc = None
        for k in range(8):
            part = None
            for a in range(4):
                j = CONV_WIDTH - 1 - (8 * a + k)
                if j < 0:
                    continue
                start = CONV_HALO - 8 - 8 * a
                term = cw_ref[j:j + 1, cs] * uwin_ref[start:start + wrows, cs]
                part = term if part is None else part + term
            part = part[8 - k:8 - k + TILE, :]
            acc = part if acc is None else acc + part
        cols.append(acc)
    c = jnp.concatenate(cols, axis=-1) + cb_ref[...]
    c = _silu(_center_norm(c) * lng_ref[...] + lnb_ref[...])
    mix_ref[:, 2 * GROUP_WIDTH:3 * GROUP_WIDTH] = _dot(c.astype(BF16), pw2bf_ref[...]).astype(mix_ref.dtype)
    convp_ref[...] = uwin_ref[CONV_HALO + TILE - (CONV_WIDTH - 1):CONV_HALO + TILE, :]
    uwin_ref[0:CONV_HALO, :] = uwin_ref[TILE:TILE + CONV_HALO, :]

    pin = h_ref[:, COL_PIN:COL_PIN + POOL_CH]
    pwin_ref[POOL_HALO:POOL_HALO + TILE, :] = pin
    pos = (t * TILE + lax.broadcasted_iota(jnp.int32, (TILE, 1), 0) + 1).astype(F32)
    pool_out = []
    for gi, w in enumerate(POOL_WINDOWS):
        cs = slice(gi * POOL_GROUP, (gi + 1) * POOL_GROUP)
        xg = pin[:, cs]
        sm = xg
        for d in range(1, w):
            sm = sm + pwin_ref[POOL_HALO - d:POOL_HALO - d + TILE, cs]
        cnt = jnp.minimum(float(w), pos)
        dd = sm / cnt - xg
        y = _dot(dd.astype(BF16), poolwbf_ref[gi]) * pscale_ref[:, cs]
        pool_out.append(y.astype(mix_ref.dtype))
    mix_ref[:, 3 * GROUP_WIDTH:4 * GROUP_WIDTH] = jnp.concatenate(pool_out, axis=-1)
    poolp_ref[...] = pwin_ref[POOL_HALO + TILE - (POOL_MAX - 1):POOL_HALO + TILE, :]
    pwin_ref[0:POOL_HALO, :] = pwin_ref[TILE:TILE + POOL_HALO, :]


def _full(shape):
    nd = len(shape)
    return pl.BlockSpec(shape, lambda b, t: (0,) * nd)


def _stack_heads(per_head):
    x = per_head.reshape((ATT_KV_HEADS, ATT_GROUP // 2, 2) + per_head.shape[1:])
    x = jnp.swapaxes(x, 1, 2)
    return x.reshape((ATT_KV_HEADS, 2, (ATT_GROUP // 2) * per_head.shape[1]) + per_head.shape[2:])


def _attention_bias_tables():
    slopes = jnp.exp2(-8.0 * jnp.arange(1, ATT_HEADS + 1, dtype=F32) / ATT_HEADS)[:, None, None]
    idx = jnp.arange(TILE)
    rel = (idx[:, None] - idx[None, :])
    dist = rel.astype(F32)[None]
    cur = jnp.where((rel >= 0)[None], -slopes * dist, -jnp.inf)
    prev = jnp.where((rel <= 0)[None], -slopes * (dist + float(TILE)), -jnp.inf)
    return _stack_heads(cur), _stack_heads(prev)


def _mixer_prompt(h, tables, bias, gn, qg, kg, sinks, cw, cb, lng, lnb, pw2, poolw, pscale):
    dmask, qdec, kdec, cdec = tables
    biasc, biasp = bias
    sinkb = _stack_heads(jnp.broadcast_to(sinks[:, None, None], (ATT_HEADS, TILE, 128)))
    qg = jnp.tile(qg, 2)
    kg = jnp.tile(kg, 2)
    in_specs = [pl.BlockSpec((TILE, IN_WIDTH), lambda b, t: (b * TILES_PER_SEQ + t, 0)),
                _full(dmask.shape), _full(qdec.shape), _full(kdec.shape), _full(cdec.shape),
                _full((1, GROUP_WIDTH)), _full((1, 128)), _full((1, 128)),
                _full(sinkb.shape), _full(biasc.shape), _full(biasp.shape),
                _full((CONV_WIDTH, CONV_CH)), _full((1, CONV_CH)), _full((1, CONV_CH)), _full((1, CONV_CH)),
                _full((CONV_CH, GROUP_WIDTH)), _full((4, POOL_GROUP, POOL_GROUP)), _full((1, POOL_CH))]
    out_specs = [pl.BlockSpec((TILE, 4 * GROUP_WIDTH), lambda b, t: (b * TILES_PER_SEQ + t, 0)),
                 pl.BlockSpec((None, RET_HEADS, RET_HEAD_DIM, RET_HEAD_DIM), lambda b, t: (b, 0, 0, 0)),
                 pl.BlockSpec((None, WINDOW, 128), lambda b, t: (b, 0, 0)),
                 pl.BlockSpec((None, WINDOW, 128), lambda b, t: (b, 0, 0)),
                 pl.BlockSpec((None, CONV_WIDTH - 1, CONV_CH), lambda b, t: (b, 0, 0)),
                 pl.BlockSpec((None, POOL_MAX - 1, POOL_CH), lambda b, t: (b, 0, 0))]
    out_shape = [jax.ShapeDtypeStruct((NTOK, 4 * GROUP_WIDTH), BF16),
                 jax.ShapeDtypeStruct((BATCH, RET_HEADS, RET_HEAD_DIM, RET_HEAD_DIM), F32),
                 jax.ShapeDtypeStruct((BATCH, WINDOW, 128), F32),
                 jax.ShapeDtypeStruct((BATCH, WINDOW, 128), F32),
                 jax.ShapeDtypeStruct((BATCH, CONV_WIDTH - 1, CONV_CH), F32),
                 jax.ShapeDtypeStruct((BATCH, POOL_MAX - 1, POOL_CH), F32)]
    scratch = [pltpu.VMEM((RET_HEADS, RET_HEAD_DIM, RET_HEAD_DIM), F32),
               pltpu.VMEM((TILE, 128), F32), pltpu.VMEM((TILE, 128), F32),
               pltpu.VMEM((CONV_HALO + TILE, CONV_CH), F32),
               pltpu.VMEM((POOL_HALO + TILE, POOL_CH), F32),
               pltpu.VMEM((CONV_CH, GROUP_WIDTH), BF16),
               pltpu.VMEM((4, POOL_GROUP, POOL_GROUP), BF16)]
    return pl.pallas_call(
        _mixer_prompt_kernel,
        grid=(BATCH, TILES_PER_SEQ),
        in_specs=in_specs, out_specs=out_specs, out_shape=out_shape, scratch_shapes=scratch,
        compiler_params=_params(("arbitrary", "arbitrary")),
        name="mixer_prompt",
    )(h, dmask, qdec, kdec, cdec, gn.reshape(1, -1), qg.reshape(1, -1), kg.reshape(1, -1), sinkb, biasc, biasp,
      cw, cb.reshape(1, -1), lng.reshape(1, -1), lnb.reshape(1, -1), pw2, poolw, pscale.reshape(1, -1))


def _mixer_sample_kernel(row_ref, hs_ref, s0_ref, ck_ref, cv_ref, cst_ref, pst_ref, gdec_ref, slope_ref,
                         gn_ref, qg_ref, kg_ref, sink_ref,
                         cw_ref, cb_ref, lng_ref, lnb_ref, pw2_ref, poolw_ref, pscale_ref,
                         ra_ref, mix_ref, ret_ref, kwin_ref, vwin_ref, convs_ref, pools_ref):
    b = pl.program_id(0)
    nb = pl.num_programs(0)

    for hh in range(RET_HEADS):
        lo = hh * RET_HEAD_DIM
        hi = lo + RET_HEAD_DIM
        q = row_ref[:, COL_RQ + lo:COL_RQ + hi]
        k = row_ref[:, COL_RK + lo:COL_RK + hi] * (RET_HEAD_DIM ** -0.5)
        v = row_ref[:, COL_RV + lo:COL_RV + hi]
        gate = row_ref[:, COL_RG + lo:COL_RG + hi]
        gdec = gdec_ref[hh]
        s_old = s0_ref[hh]
        q8 = jnp.broadcast_to(q, (8, RET_HEAD_DIM)).astype(BF16)
        qs = _dot(q8, s_old.astype(BF16))[0:1, :]
        a = jnp.sum(q * k, axis=-1, keepdims=True)
        o = a * v + qs * gdec
        k8 = jnp.where(lax.broadcasted_iota(jnp.int32, (8, RET_HEAD_DIM), 0) == 0, k, 0.0).astype(BF16)
        v8 = jnp.broadcast_to(v, (8, RET_HEAD_DIM)).astype(BF16)
        ret_ref[hh] = s_old * gdec + _dot_tn(k8, v8)
        y = _center_norm(o) * gn_ref[:, lo:hi]
        ra_ref[:, lo:hi] = _silu(gate) * y

    knew = row_ref[:, COL_AK:COL_AK + 128]
    vnew = row_ref[:, COL_AV:COL_AV + 128]
    scale = ATT_HEAD_DIM ** -0.5
    dist = (WINDOW - lax.broadcasted_iota(jnp.int32, (1, WINDOW), 1)).astype(F32)
    kn_parts = []
    for g in range(ATT_KV_HEADS):
        sl = slice(g * ATT_HEAD_DIM, (g + 1) * ATT_HEAD_DIM)
        kng = _rms(knew[:, sl], kg_ref[...])
        kn_parts.append(kng)
        vng = vnew[:, sl]
        q0 = COL_AQ + g * ATT_GROUP * ATT_HEAD_DIM
        qs8 = jnp.concatenate(
            [row_ref[:, q0 + i * ATT_HEAD_DIM:q0 + (i + 1) * ATT_HEAD_DIM] for i in range(ATT_GROUP)], axis=0)
        qn = _rms(qs8, qg_ref[...])
        slope = slope_ref[g]
        sink = sink_ref[g]
        s_past = _dot_nt(qn.astype(BF16), ck_ref[:, sl].astype(BF16)) * scale - slope * dist
        s_new = jnp.sum(qn * kng, axis=-1, keepdims=True) * scale
        mx = jnp.maximum(jnp.maximum(jnp.max(s_past, axis=-1, keepdims=True), s_new), sink)
        e_past = jnp.exp(s_past - mx)
        e_new = jnp.exp(s_new - mx)
        den = jnp.sum(e_past, axis=-1, keepdims=True) + e_new + jnp.exp(sink - mx)
        inv = 1.0 / den
        o = _dot((e_past * inv).astype(BF16), cv_ref[:, sl].astype(BF16)) + (e_new * inv) * vng
        for i in range(ATT_GROUP):
            c0 = GROUP_WIDTH + (g * ATT_GROUP + i) * ATT_HEAD_DIM
            ra_ref[:, c0:c0 + ATT_HEAD_DIM] = o[i:i + 1, :]
    kwin_ref[0:WINDOW - 1, :] = ck_ref[1:WINDOW, :]
    kwin_ref[WINDOW - 1:WINDOW, :] = jnp.concatenate(kn_parts, axis=-1)
    vwin_ref[0:WINDOW - 1, :] = cv_ref[1:WINDOW, :]
    vwin_ref[WINDOW - 1:WINDOW, :] = vnew

    @pl.when(b == nb - 1)
    def _():
        u = hs_ref[:, COL_CA:COL_CA + CONV_CH] * _sigmoid(hs_ref[:, COL_CG:COL_CG + CONV_CH])
        c = cw_ref[CONV_WIDTH - 1:CONV_WIDTH, :] * u
        for j in range(CONV_WIDTH - 1):
            c = c + cw_ref[j:j + 1, :] * cst_ref[j]
        c = c + cb_ref[...]
        c = _silu(_center_norm(c) * lng_ref[...] + lnb_ref[...])
        mix_ref[:, 0:GROUP_WIDTH] = _dot(c.astype(BF16), pw2_ref[...].astype(BF16)).astype(mix_ref.dtype)
        for j in range(CONV_WIDTH - 2):
            convs_ref[j] = cst_ref[j + 1]
        convs_ref[CONV_WIDTH - 2] = u

        pin = hs_ref[:, COL_PIN:COL_PIN + POOL_CH]
        npast = POOL_MAX - 1
        for gi, w in enumerate(POOL_WINDOWS):
            cs = slice(gi * POOL_GROUP, (gi + 1) * POOL_GROUP)
            xg = pin[:, cs]
            sm = xg
            for d in range(1, w):
                sm = sm + pst_ref[npast - d, :, cs]
            dd = sm / float(w) - xg
            y = _dot(dd.astype(BF16), poolw_ref[gi].astype(BF16)) * pscale_ref[:, cs]
            mix_ref[:, GROUP_WIDTH + gi * POOL_GROUP:GROUP_WIDTH + (gi + 1) * POOL_GROUP] = y.astype(mix_ref.dtype)
        for j in range(npast - 1):
            pools_ref[j] = pst_ref[j + 1]
        pools_ref[npast - 1] = pin


def _mixer_sample(layer, hs, state_ret, cache_k, cache_v, cst, pst, gdec, slopes,
                  gn, qg, kg, sinks, cw, cb, lng, lnb, pw2, poolw, pscale):
    nb = DEC_BATCH

    def full(shape):
        nd = len(shape)
        return pl.BlockSpec(shape, lambda b: (0,) * nd)

    in_specs = [pl.BlockSpec((None, 1, IN_WIDTH), lambda b: (b, 0, 0)),
                full((nb, IN_WIDTH)),
                pl.BlockSpec((None, None, RET_HEADS, RET_HEAD_DIM, RET_HEAD_DIM), lambda b: (layer, b, 0, 0, 0)),
                pl.BlockSpec((None, None, WINDOW, 128), lambda b: (layer, b, 0, 0)),
                pl.BlockSpec((None, None, WINDOW, 128), lambda b: (layer, b, 0, 0)),
                full((CONV_WIDTH - 1, nb, CONV_CH)), full((POOL_MAX - 1, nb, POOL_CH)),
                full((RET_HEADS, 1, RET_HEAD_DIM)), full((ATT_KV_HEADS, ATT_GROUP, 1)),
                full((1, GROUP_WIDTH)), full((1, ATT_HEAD_DIM)), full((1, ATT_HEAD_DIM)),
                full((ATT_KV_HEADS, ATT_GROUP, 1)),
                full((CONV_WIDTH, CONV_CH)), full((1, CONV_CH)), full((1, CONV_CH)), full((1, CONV_CH)),
                full((CONV_CH, GROUP_WIDTH)), full((4, POOL_GROUP, POOL_GROUP)), full((1, POOL_CH))]
    out_specs = [pl.BlockSpec((None, 1, 2 * GROUP_WIDTH), lambda b: (b, 0, 0)),
                 full((nb, 2 * GROUP_WIDTH)),
                 pl.BlockSpec((None, RET_HEADS, RET_HEAD_DIM, RET_HEAD_DIM), lambda b: (b, 0, 0, 0)),
                 pl.BlockSpec((None, WINDOW, 128), lambda b: (b, 0, 0)),
                 pl.BlockSpec((None, WINDOW, 128), lambda b: (b, 0, 0)),
                 full((CONV_WIDTH - 1, nb, CONV_CH)), full((POOL_MAX - 1, nb, POOL_CH))]
    out_shape = [jax.ShapeDtypeStruct((nb, 1, 2 * GROUP_WIDTH), F32),
                 jax.ShapeDtypeStruct((nb, 2 * GROUP_WIDTH), BF16),
                 jax.ShapeDtypeStruct((nb, RET_HEADS, RET_HEAD_DIM, RET_HEAD_DIM), F32),
                 jax.ShapeDtypeStruct((nb, WINDOW, 128), F32),
                 jax.ShapeDtypeStruct((nb, WINDOW, 128), F32),
                 jax.ShapeDtypeStruct((CONV_WIDTH - 1, nb, CONV_CH), F32),
                 jax.ShapeDtypeStruct((POOL_MAX - 1, nb, POOL_CH), F32)]
    ra, cp, *states = pl.pallas_call(
        _mixer_sample_kernel,
        grid=(nb,),
        in_specs=in_specs, out_specs=out_specs, out_shape=out_shape,
        compiler_params=_params(("arbitrary",)),
        name="mixer_sample",
    )(hs.reshape(nb, 1, IN_WIDTH), hs, state_ret, cache_k, cache_v, cst, pst, gdec, slopes,
      gn.reshape(1, -1), qg.reshape(1, -1), kg.reshape(1, -1), sinks.reshape(ATT_KV_HEADS, ATT_GROUP, 1),
      cw, cb.reshape(1, -1), lng.reshape(1, -1), lnb.reshape(1, -1), pw2, poolw, pscale.reshape(1, -1))
    mix = jnp.concatenate([ra.reshape(nb, 2 * GROUP_WIDTH).astype(BF16), cp], axis=-1)
    return (mix, *states)


def _retention_tables():
    c = RET_CHUNK
    log_g = jnp.log1p(-jnp.exp2(-5.0 - jnp.arange(RET_HEADS, dtype=F32)))
    idx = jnp.arange(c, dtype=F32)
    rel = idx[:, None] - idx[None, :]
    dmask = jnp.where(rel >= 0, jnp.exp(log_g[:, None, None] * jnp.maximum(rel, 0.0)), 0.0)
    ones = jnp.ones((1, 1, RET_HEAD_DIM), F32)
    qdec = jnp.exp(log_g[:, None] * (idx + 1.0))[:, :, None] * ones
    kdec = jnp.exp(log_g[:, None] * (c - 1.0 - idx))[:, :, None] * ones
    cdec = jnp.exp(log_g * c)[:, None, None] * jnp.ones((1, RET_HEAD_DIM, RET_HEAD_DIM), F32)
    gdec = jnp.exp(log_g)[:, None, None] * ones
    return (dmask, qdec, kdec, cdec), gdec


def kernel(x_prompt, x_sample, state_ret, cache_k_win, cache_v_win, state_conv, state_pool, state_ffn,
           norm1_g, w_in, ret_gn_g, q_norm_g, k_norm_g, att_sinks, conv_dw_w, conv_dw_b, conv_ln_g, conv_ln_b,
           conv_pw2, pool_w, pool_scale, w_out, norm2_g, ffn_up, ffn_dw_w, ffn_dw_b, ffn_down):
    tables, gdec = _retention_tables()
    bias = _attention_bias_tables()
    slopes =jnp.exp2(-8.0 * jnp.arange(1, ATT_HEADS + 1, dtype=F32) / ATT_HEADS).reshape(ATT_KV_HEADS, ATT_GROUP, 1)
    cache_k = cache_k_win.reshape(DEPTH, DEC_BATCH, WINDOW, 128)
    cache_v = cache_v_win.reshape(DEPTH, DEC_BATCH, WINDOW, 128)

    xp = x_prompt.reshape(NTOK, D_MODEL)
    xs = x_sample.reshape(DEC_BATCH, D_MODEL)
    outs_p = [[] for _ in range(6)]
    outs_s = [[] for _ in range(6)]
    for l in range(DEPTH):
        xnp = _rmsnorm(xp, norm1_g[l], 256)
        xns = _rmsnorm(xs, norm1_g[l], DEC_BATCH)
        hp, hs = _matmul(xnp, xns, w_in, l, 512, 768, name="w_in")
        mixp, ret_p, kwin_p, vwin_p, conv_p, pool_p = _mixer_prompt(
            hp, tables, bias, ret_gn_g[l], q_norm_g[l], k_norm_g[l], att_sinks[l], conv_dw_w[l], conv_dw_b[l],
            conv_ln_g[l], conv_ln_b[l], conv_pw2[l], pool_w[l], pool_scale[l])
        cst = jnp.swapaxes(state_conv[l], 0, 1)
        pst = jnp.swapaxes(state_pool[l], 0, 1)
        mixs, ret_s, kwin_s, vwin_s, conv_s, pool_s = _mixer_sample(
            l, hs, state_ret, cache_k, cache_v, cst, pst, gdec, slopes,
            ret_gn_g[l], q_norm_g[l], k_norm_g[l], att_sinks[l], conv_dw_w[l], conv_dw_b[l],
            conv_ln_g[l], conv_ln_b[l], conv_pw2[l], pool_w[l], pool_scale[l])
        xp, xs = _matmul(mixp, mixs, w_out, l, 1024, 512, res=(xp, xs), name="w_out")
        xnp = _rmsnorm(xp, norm2_g[l], 256)
        xns = _rmsnorm(xs, norm2_g[l], DEC_BATCH)
        fst = jnp.swapaxes(state_ffn[l], 0, 1)
        actp, acts, ffn_p, ffn_s = _ffn_up(xnp, xns, ffn_up, l, ffn_dw_w[l], ffn_dw_b[l], fst, 1024, 256)
        xp, xs = _matmul(actp, acts, ffn_down, l, 512, 256, res=(xp, xs), name="ffn_down")
        for lst, v in zip(outs_p, (ret_p, kwin_p.reshape(BATCH, WINDOW, ATT_KV_HEADS, ATT_HEAD_DIM),
                                   vwin_p.reshape(BATCH, WINDOW, ATT_KV_HEADS, ATT_HEAD_DIM), conv_p, pool_p, ffn_p)):
            lst.append(v)
        for lst, v in zip(outs_s, (ret_s, kwin_s.reshape(DEC_BATCH, WINDOW, ATT_KV_HEADS, ATT_HEAD_DIM),
                                   vwin_s.reshape(DEC_BATCH, WINDOW, ATT_KV_HEADS, ATT_HEAD_DIM),
                                   jnp.swapaxes(conv_s, 0, 1), jnp.swapaxes(pool_s, 0, 1),
                                   jnp.swapaxes(ffn_s, 0, 1))):
            lst.append(v)
    y_prompt = xp.reshape(BATCH, SEQ, D_MODEL)
    y_sample = xs.reshape(DEC_BATCH, 1, D_MODEL)
    return (y_prompt, y_sample, *[jnp.stack(v) for v in outs_p], *[jnp.stack(v) for v in outs_s])
```

```python
import functools

import jax
import jax.numpy as jnp
from jax import lax
from jax.experimental import pallas as pl
from jax.experimental.pallas import tpu as pltpu

D_MODEL = 4096
BATCH = 4
SEQ = 2048
DEPTH = 4
DEC_BATCH = 32
PAST_LEN = 8192
GROUP_WIDTH = 1024
RET_HEADS = 8
RET_HEAD_DIM = 128
RET_CHUNK = 128
ATT_HEAD_DIM = 64
ATT_HEADS = 16
ATT_KV_HEADS = 2
ATT_GROUP = 8
WINDOW = 128
CONV_CH = 1024
CONV_WIDTH = 31
POOL_CH = 1024
POOL_WINDOWS = (2, 4, 8, 16)
POOL_GROUP = 256
POOL_MAX = 16
D_FF = 11008
FFN_CONV_WIDTH = 3
NORM_EPS = 1e-6
IN_WIDTH = 8448

COL_RQ, COL_RK, COL_RV, COL_RG = 0, 1024, 2048, 3072
COL_AQ, COL_AK, COL_AV = 4096, 5120, 5248
COL_CA, COL_CG, COL_PIN = 5376, 6400, 7424

NTOK = BATCH * SEQ
TILE = 128
TILES_PER_SEQ = SEQ // TILE
CONV_HALO = 32
POOL_HALO = 16
FFN_HALO = 8

BF16 = jnp.bfloat16
F32 = jnp.float32
VMEM_LIMIT = 62 * 1024 * 1024


def _dot(a, b):
    return jnp.dot(a, b, preferred_element_type=F32)


def _dot_nt(a, b):
    return lax.dot_general(a, b, (((1,), (1,)), ((), ())), preferred_element_type=F32)


def _dot_tn(a, b):
    return lax.dot_general(a, b, (((0,), (0,)), ((), ())), preferred_element_type=F32)


def _sigmoid(x):
    return 1.0 / (1.0 + jnp.exp(-x))


def _silu(x):
    return x * _sigmoid(x)


def _rms(x, g):
    return x * lax.rsqrt(jnp.mean(x * x, axis=-1, keepdims=True) + NORM_EPS) * g


def _center_norm(x):
    mu = jnp.mean(x, axis=-1, keepdims=True)
    xc = x - mu
    return xc * lax.rsqrt(jnp.mean(xc * xc, axis=-1, keepdims=True) + NORM_EPS)


def _params(sem):
    return pltpu.CompilerParams(dimension_semantics=sem, vmem_limit_bytes=VMEM_LIMIT)


def _rmsnorm_kernel(x_ref, g_ref, o_ref):
    o_ref[...] = _rms(x_ref[...], g_ref[...]).astype(o_ref.dtype)


def _rmsnorm(x, g, rows):
    m, d = x.shape
    return pl.pallas_call(
        _rmsnorm_kernel,
        grid=(m // rows,),
        in_specs=[pl.BlockSpec((rows, d), lambda i: (i, 0)),
                  pl.BlockSpec((1, d), lambda i: (0, 0))],
        out_specs=pl.BlockSpec((rows, d), lambda i: (i, 0)),
        out_shape=jax.ShapeDtypeStruct((m, d), BF16),
        compiler_params=_params(("arbitrary",)),
        name="rmsnorm",
    )(x, g.reshape(1, d))


def _mm_kernel(*refs, has_res):
    if has_res:
        xp_ref, xs_ref, w_ref, rp_ref, rs_ref, op_ref, os_ref, wbf_ref = refs
    else:
        xp_ref, xs_ref, w_ref, op_ref, os_ref, wbf_ref = refs

    @pl.when(pl.program_id(1) == 0)
    def _():
        wbf_ref[...] = w_ref[...].astype(BF16)
        acc = _dot(xs_ref[...], wbf_ref[...])
        if has_res:
            acc = acc + rs_ref[...]
        os_ref[...] = acc.astype(os_ref.dtype)

    acc = _dot(xp_ref[...], wbf_ref[...])
    if has_res:
        acc = acc + rp_ref[...]
    op_ref[...] = acc.astype(op_ref.dtype)


def _matmul(xp, xs, w, layer, bm, bn, res=None, name="matmul"):
    m, k = xp.shape
    s = xs.shape[0]
    n = w.shape[2]
    in_specs = [pl.BlockSpec((bm, k), lambda j, i: (i, 0)),
                pl.BlockSpec((s, k), lambda j, i: (0, 0)),
                pl.BlockSpec((None, k, bn), lambda j, i: (layer, 0, j))]
    args = [xp, xs, w]
    if res is not None:
        in_specs += [pl.BlockSpec((bm, bn), lambda j, i: (i, j)),
                     pl.BlockSpec((s, bn), lambda j, i: (0, j))]
        args += list(res)
    return pl.pallas_call(
        functools.partial(_mm_kernel, has_res=res is not None),
        grid=(pl.cdiv(n, bn), m // bm),
        in_specs=in_specs,
        out_specs=[pl.BlockSpec((bm, bn), lambda j, i: (i, j)),
                   pl.BlockSpec((s, bn), lambda j, i: (0, j))],
        out_shape=[jax.ShapeDtypeStruct((m, n), F32), jax.ShapeDtypeStruct((s, n), F32)],
        scratch_shapes=[pltpu.VMEM((k, bn), BF16)],
        compiler_params=_params(("arbitrary", "arbitrary")),
        name=name,
    )(*args)


MLP_BM = 1024
MLP_BF = 256
MLP_NF = D_FF // MLP_BF
MLP_RES = 256
MLP_DOWN_CHUNK = 512


def _mlp_kernel(xn_ref, xns_ref, xres_ref, xsres_ref, wg_ref, wv_ref, wd_ref, dw_ref, db_ref, past_ref,
                out_ref, outs_ref, ffnp_ref, ffns_ref,
                g_ref, v_ref, act_ref, carry_ref, gs_ref, vs_ref, acts_ref):
    m = pl.program_id(0)
    s = pl.program_id(1)
    bm = MLP_BM
    w0 = dw_ref[0:1, :]
    w1 = dw_ref[1:2, :]
    w2 = dw_ref[2:3, :]
    bias = db_ref[...]
    cslot = jnp.where(s == 0, MLP_NF, s - 1)

    @pl.when(s == 0)
    def _():
        out_ref[...] = jnp.zeros(out_ref.shape, F32)
        g_ref[...] = jnp.zeros(g_ref.shape, F32)
        v_ref[...] = jnp.zeros(v_ref.shape, F32)
        act_ref[...] = jnp.zeros(act_ref.shape, BF16)

    @pl.when((s == 0) & (m == 0))
    def _():
        outs_ref[...] = xsres_ref[...]
        gs_ref[...] = jnp.zeros(gs_ref.shape, F32)
        vs_ref[...] = jnp.zeros(vs_ref.shape, F32)
        acts_ref[...] = jnp.zeros(acts_ref.shape, BF16)

    for c in range(D_MODEL // MLP_RES):
        @pl.when(s == c)
        def _(c=c):
            out_ref[:, c * MLP_RES:(c + 1) * MLP_RES] += xres_ref[...]

    def down(act, acc_ref):
        for nc in range(D_MODEL // MLP_DOWN_CHUNK):
            cs = slice(nc * MLP_DOWN_CHUNK, (nc + 1) * MLP_DOWN_CHUNK)
            acc_ref[:, cs] += _dot(act, wd_ref[:, cs].astype(BF16))

    def up(x_ref):
        g = None
        v = None
        for kc in range(4):
            ks = slice(kc * 1024, (kc + 1) * 1024)
            x = x_ref[:, ks]
            pg = _dot(x, wg_ref[ks, :].astype(BF16))
            pv = _dot(x, wv_ref[ks, :].astype(BF16))
            g = pg if g is None else g + pg
            v = pv if v is None else v + pv
        return g, v

    wslot = s % 2
    rslot = 1 - wslot

    def sample_step(do_up, do_epilogue):
        down(acts_ref[rslot], outs_ref)
        if do_epilogue:
            gs = gs_ref[...]
            c = w0 * past_ref[0] + w1 * past_ref[1] + w2 * gs + bias
            acts_ref[wslot] = (_silu(c) * vs_ref[...]).astype(BF16)
            ffns_ref[0] = past_ref[1]
            ffns_ref[1] = gs
        if do_up:
            g_new, v_new = up(xns_ref)
            gs_ref[...] = g_new
            vs_ref[...] = v_new

    def prompt_step(do_up, do_epilogue):
        if do_epilogue:
            @pl.when(m % (SEQ // bm) == 0)
            def _():
                g_ref[0:FFN_HALO, :] = jnp.zeros((FFN_HALO, MLP_BF), F32)

            @pl.when(m % (SEQ // bm) != 0)
            def _():
                g_ref[0:FFN_HALO, :] = carry_ref[cslot]

        if do_up:
            g_new, v_new = up(xn_ref)
        down(act_ref[rslot], out_ref)
        if do_epilogue:
            g = g_ref[FFN_HALO:FFN_HALO + bm, :]
            g1 = g_ref[FFN_HALO - 1:FFN_HALO - 1 + bm, :]
            g2 = g_ref[FFN_HALO - 2:FFN_HALO - 2 + bm, :]
            c = w0 * g2 + w1 * g1 + w2 * g + bias
            act_ref[wslot] = (_silu(c) * v_ref[...]).astype(BF16)
            ffnp_ref[...] = g_ref[FFN_HALO + bm - 2:FFN_HALO + bm, :]
            carry_ref[cslot] = g_ref[bm:bm + FFN_HALO, :]
        if do_up:
            g_ref[FFN_HALO:FFN_HALO + bm, :] = g_new
            v_ref[...] = v_new

    for cond, do_up, do_epilogue in ((s < MLP_NF, True, True), (s == MLP_NF, False, True),
                                     (s == MLP_NF + 1, False, False)):
        @pl.when(cond & (m == 0))
        def _(do_up=do_up, do_epilogue=do_epilogue):
            sample_step(do_up, do_epilogue)

        @pl.when(cond)
        def _(do_up=do_up, do_epilogue=do_epilogue):
            prompt_step(do_up, do_epilogue)


def _mlp(xnp, xns, xp, xs, w_up, w_down, layer, dw_w, dw_b, past_t):
    m, d = xnp.shape
    sb = xns.shape[0]
    bm, bf, nf = MLP_BM, MLP_BF, MLP_NF
    tiles_per_seq = SEQ // bm
    nres = d // MLP_RES
    one = pl.Buffered(1)

    def prev(s):
        return jnp.clip(s - 1, 0, nf - 1)

    def prev2(s):
        return jnp.clip(s - 2, 0, nf - 1)

    in_specs = [pl.BlockSpec((bm, d), lambda i, s: (i, 0), pipeline_mode=one),
                pl.BlockSpec((sb, d), lambda i, s: (0, 0)),
                pl.BlockSpec((bm, MLP_RES), lambda i, s: (i, jnp.minimum(s, nres - 1))),
                pl.BlockSpec((sb, d), lambda i, s: (0, 0)),
                pl.BlockSpec((None, d, bf), lambda i, s: (layer, 0, jnp.minimum(s, nf - 1))),
                pl.BlockSpec((None, d, bf), lambda i, s: (layer, 0, jnp.minimum(s, nf - 1) + nf)),
                pl.BlockSpec((None, bf, d), lambda i, s: (layer, prev2(s), 0)),
                pl.BlockSpec((FFN_CONV_WIDTH, bf), lambda i, s: (0, prev(s))),
                pl.BlockSpec((1, bf), lambda i, s: (0, prev(s))),
                pl.BlockSpec((2, sb, bf), lambda i, s: (0, 0, prev(s)))]
    out_specs = [pl.BlockSpec((bm, d), lambda i, s: (i, 0), pipeline_mode=one),
                 pl.BlockSpec((sb, d), lambda i, s: (0, 0)),
                 pl.BlockSpec((None, 2, bf), lambda i, s: (i, 0, prev(s))),
                 pl.BlockSpec((2, sb, bf), lambda i, s: (0, 0, jnp.where(i == 0, prev(s), nf - 1)))]
    out_shape = [jax.ShapeDtypeStruct((m, d), F32),
                 jax.ShapeDtypeStruct((sb, d), F32),
                 jax.ShapeDtypeStruct((m // bm, 2, D_FF), F32),
                 jax.ShapeDtypeStruct((2, sb, D_FF), F32)]
    scratch = [pltpu.VMEM((FFN_HALO + bm, bf), F32), pltpu.VMEM((bm, bf), F32), pltpu.VMEM((2, bm, bf), BF16),
               pltpu.VMEM((nf + 1, FFN_HALO, bf), F32),
               pltpu.VMEM((sb, bf), F32), pltpu.VMEM((sb, bf), F32), pltpu.VMEM((2, sb, bf), BF16)]
    yp, ys, tails, ffn_s = pl.pallas_call(
        _mlp_kernel,
        grid=(m // bm, nf + 2),
        in_specs=in_specs, out_specs=out_specs, out_shape=out_shape, scratch_shapes=scratch,
        compiler_params=_params(("arbitrary", "arbitrary")),
        name="conv_ffn",
    )(xnp, xns, xp, xs, w_up, w_up, w_down, dw_w, dw_b.reshape(1, D_FF), past_t)
    return yp, ys, tails[tiles_per_seq - 1::tiles_per_seq], ffn_s


def _mixer_prompt_kernel(h_ref, dmask_ref, qdec_ref, kdec_ref, cdec_ref, gn_ref, qg_ref, kg_ref,
                         sinkb_ref, biasc_ref, biasp_ref,
                         cw_ref, cb_ref, lng_ref, lnb_ref, pw2_ref, poolw_ref, pscale_ref,
                         mix_ref, ret_ref, kwin_ref, vwin_ref, convp_ref, poolp_ref,
                         s_ref, kprev_ref, vprev_ref, uwin_ref, pwin_ref, pw2bf_ref, poolwbf_ref):
    b = pl.program_id(0)
    t = pl.program_id(1)

    @pl.when((b == 0) & (t == 0))
    def _():
        pw2bf_ref[...] = pw2_ref[...].astype(BF16)
        poolwbf_ref[...] = poolw_ref[...].astype(BF16)

    @pl.when(t == 0)
    def _():
        s_ref[...] = jnp.zeros(s_ref.shape, F32)
        kprev_ref[...] = jnp.zeros(kprev_ref.shape, F32)
        vprev_ref[...] = jnp.zeros(vprev_ref.shape, F32)
        uwin_ref[0:CONV_HALO, :] = jnp.zeros((CONV_HALO, CONV_CH), F32)
        pwin_ref[0:POOL_HALO, :] = jnp.zeros((POOL_HALO, POOL_CH), F32)

    ret_out, ret_state = [], []
    for hh in range(RET_HEADS):
        lo = hh * RET_HEAD_DIM
        hi = lo + RET_HEAD_DIM
        q = h_ref[:, COL_RQ + lo:COL_RQ + hi]
        k = h_ref[:, COL_RK + lo:COL_RK + hi] * (RET_HEAD_DIM ** -0.5)
        v = h_ref[:, COL_RV + lo:COL_RV + hi]
        qb = q.astype(BF16)
        vb = v.astype(BF16)
        a = _dot_nt(qb, k.astype(BF16)) * dmask_ref[hh]
        s_old = s_ref[hh]
        o = _dot(a.astype(BF16), vb) + _dot(qb, s_old.astype(BF16)) * qdec_ref[hh]
        ret_state.append(s_old * cdec_ref[hh] + _dot_tn((k * kdec_ref[hh]).astype(BF16), vb))
        y = _center_norm(o) * gn_ref[:, lo:hi]
        gate = h_ref[:, COL_RG + lo:COL_RG + hi]
        ret_out.append((_silu(gate) * y).astype(mix_ref.dtype))
    mix_ref[:, 0:GROUP_WIDTH] = jnp.concatenate(ret_out, axis=-1)
    for hh in range(RET_HEADS):
        s_ref[hh] = ret_state[hh]
        ret_ref[hh] = ret_state[hh]

    lane = lax.broadcasted_iota(jnp.int32, (1, 128), 1)
    half_mask = (lane < ATT_HEAD_DIM, lane >= ATT_HEAD_DIM)

    def pair_rms(x, gain):
        sq = x * x
        ms = [jnp.sum(jnp.where(hm, sq, 0.0), axis=-1, keepdims=True) * (1.0 / ATT_HEAD_DIM) for hm in half_mask]
        r = jnp.where(half_mask[0], lax.rsqrt(ms[0] + NORM_EPS), lax.rsqrt(ms[1] + NORM_EPS))
        return x * r * gain

    kn = pair_rms(h_ref[:, COL_AK:COL_AK + 128], kg_ref[...])
    vc = h_ref[:, COL_AV:COL_AV + 128]
    kwin_ref[...] = kn
    vwin_ref[...] = vc
    kp = kprev_ref[...]
    vp = vprev_ref[...]
    kprev_ref[...] = kn
    vprev_ref[...] = vc
    swapped = {id(x): pltpu.roll(x, ATT_HEAD_DIM, axis=1) for x in (kn, vc, kp, vp)}

    def half_of(x, src, dst):
        y = x if src == dst else swapped[id(x)]
        return jnp.where(half_mask[dst], y, 0.0).astype(BF16)

    scale = ATT_HEAD_DIM ** -0.5
    att_out = []
    for g in range(ATT_KV_HEADS):
        q0 = COL_AQ + g * ATT_GROUP * ATT_HEAD_DIM
        qst = jnp.concatenate([h_ref[:, q0 + p * 128:q0 + (p + 1) * 128] for p in range(ATT_GROUP // 2)], axis=0)
        qn = pair_rms(qst, qg_ref[...]).astype(BF16)
        o = None
        for par in range(2):
            s_cur = _dot_nt(qn, half_of(kn, g, par)) * scale + biasc_ref[g, par]
            s_prev = _dot_nt(qn, half_of(kp, g, par)) * scale + biasp_ref[g, par]
            s_prev = jnp.where(t > 0, s_prev, -jnp.inf)
            sink = sinkb_ref[g, par]
            mx = jnp.max(jnp.maximum(jnp.maximum(s_cur, s_prev), sink), axis=-1, keepdims=True)
            e_cur = jnp.exp(s_cur - mx)
            e_prev = jnp.exp(s_prev - mx)
            den = jnp.sum(e_cur + e_prev, axis=-1, keepdims=True) + jnp.exp(sink[:, 0:1] - mx)
            inv = 1.0 / den
            od = (_dot((e_cur * inv).astype(BF16), half_of(vc, g, par))
                  + _dot((e_prev * inv).astype(BF16), half_of(vp, g, par)))
            o = od if o is None else o + od
        att_out += [o[p * 128:(p + 1) * 128, :].astype(mix_ref.dtype) for p in range(ATT_GROUP // 2)]
    mix_ref[:, GROUP_WIDTH:2 * GROUP_WIDTH] = jnp.concatenate(att_out, axis=-1)

    u = h_ref[:, COL_CA:COL_CA + CONV_CH] * _sigmoid(h_ref[:, COL_CG:COL_CG + CONV_CH])
    uwin_ref[CONV_HALO:CONV_HALO + TILE, :] = u
    wrows = TILE + 8
    cols = []
    for cc in range(CONV_CH // 128):
        cs = slice(cc * 128, (cc + 1) * 128)
        acc = None
        for k in range(8):
            part = None
            for a in range(4):
                j = CONV_WIDTH - 1 - (8 * a + k)
                if j < 0:
                    continue
                start = CONV_HALO - 8 - 8 * a
                term = cw_ref[j:j + 1, cs] * uwin_ref[start:start + wrows, cs]
                part = term if part is None else part + term
            part = part[8 - k:8 - k + TILE, :]
            acc = part if acc is None else acc + part
        cols.append(acc)
    c = jnp.concatenate(cols, axis=-1) + cb_ref[...]
    c = _silu(_center_norm(c) * lng_ref[...] + lnb_ref[...])
    mix_ref[:, 2 * GROUP_WIDTH:3 * GROUP_WIDTH] = _dot(c.astype(BF16), pw2bf_ref[...]).astype(mix_ref.dtype)
    convp_ref[...] = uwin_ref[CONV_HALO + TILE - (CONV_WIDTH - 1):CONV_HALO + TILE, :]
    uwin_ref[0:CONV_HALO, :] = uwin_ref[TILE:TILE + CONV_HALO, :]

    pin = h_ref[:, COL_PIN:COL_PIN + POOL_CH]
    pwin_ref[POOL_HALO:POOL_HALO + TILE, :] = pin
    pos = (t * TILE + lax.broadcasted_iota(jnp.int32, (TILE, 1), 0) + 1).astype(F32)
    pool_out = []
    for gi, w in enumerate(POOL_WINDOWS):
        cs = slice(gi * POOL_GROUP, (gi + 1) * POOL_GROUP)
        xg = pin[:, cs]
        sm = xg
        for d in range(1, w):
            sm = sm + pwin_ref[POOL_HALO - d:POOL_HALO - d + TILE, cs]
        cnt = jnp.minimum(float(w), pos)
        dd = sm / cnt - xg
        y = _dot(dd.astype(BF16), poolwbf_ref[gi]) * pscale_ref[:, cs]
        pool_out.append(y.astype(mix_ref.dtype))
    mix_ref[:, 3 * GROUP_WIDTH:4 * GROUP_WIDTH] = jnp.concatenate(pool_out, axis=-1)
    poolp_ref[...] = pwin_ref[POOL_HALO + TILE - (POOL_MAX - 1):POOL_HALO + TILE, :]
    pwin_ref[0:POOL_HALO, :] = pwin_ref[TILE:TILE + POOL_HALO, :]


def _full(shape):
    nd = len(shape)
    return pl.BlockSpec(shape, lambda b, t: (0,) * nd)


def _stack_heads(per_head):
    x = per_head.reshape((ATT_KV_HEADS, ATT_GROUP // 2, 2) + per_head.shape[1:])
    x = jnp.swapaxes(x, 1, 2)
    return x.reshape((ATT_KV_HEADS, 2, (ATT_GROUP // 2) * per_head.shape[1]) + per_head.shape[2:])


def _attention_bias_tables():
    slopes = jnp.exp2(-8.0 * jnp.arange(1, ATT_HEADS + 1, dtype=F32) / ATT_HEADS)[:, None, None]
    idx = jnp.arange(TILE)
    rel = (idx[:, None] - idx[None, :])
    dist = rel.astype(F32)[None]
    cur = jnp.where((rel >= 0)[None], -slopes * dist, -jnp.inf)
    prev = jnp.where((rel <= 0)[None], -slopes * (dist + float(TILE)), -jnp.inf)
    return _stack_heads(cur), _stack_heads(prev)


def _mixer_prompt(h, tables, bias, gn, qg, kg, sinks, cw, cb, lng, lnb, pw2, poolw, pscale):
    dmask, qdec, kdec, cdec = tables
    biasc, biasp = bias
    sinkb = _stack_heads(jnp.broadcast_to(sinks[:, None, None], (ATT_HEADS, TILE, 128)))
    qg = jnp.tile(qg, 2)
    kg = jnp.tile(kg, 2)
    in_specs = [pl.BlockSpec((TILE, IN_WIDTH), lambda b, t: (b * TILES_PER_SEQ + t, 0)),
                _full(dmask.shape), _full(qdec.shape), _full(kdec.shape), _full(cdec.shape),
                _full((1, GROUP_WIDTH)), _full((1, 128)), _full((1, 128)),
                _full(sinkb.shape), _full(biasc.shape), _full(biasp.shape),
                _full((CONV_WIDTH, CONV_CH)), _full((1, CONV_CH)), _full((1, CONV_CH)), _full((1, CONV_CH)),
                _full((CONV_CH, GROUP_WIDTH)), _full((4, POOL_GROUP, POOL_GROUP)), _full((1, POOL_CH))]
    out_specs = [pl.BlockSpec((TILE, 4 * GROUP_WIDTH), lambda b, t: (b * TILES_PER_SEQ + t, 0)),
                 pl.BlockSpec((None, RET_HEADS, RET_HEAD_DIM, RET_HEAD_DIM), lambda b, t: (b, 0, 0, 0)),
                 pl.BlockSpec((None, WINDOW, 128), lambda b, t: (b, 0, 0)),
                 pl.BlockSpec((None, WINDOW, 128), lambda b, t: (b, 0, 0)),
                 pl.BlockSpec((None, CONV_WIDTH - 1, CONV_CH), lambda b, t: (b, 0, 0)),
                 pl.BlockSpec((None, POOL_MAX - 1, POOL_CH), lambda b, t: (b, 0, 0))]
    out_shape = [jax.ShapeDtypeStruct((NTOK, 4 * GROUP_WIDTH), BF16),
                 jax.ShapeDtypeStruct((BATCH, RET_HEADS, RET_HEAD_DIM, RET_HEAD_DIM), F32),
                 jax.ShapeDtypeStruct((BATCH, WINDOW, 128), F32),
                 jax.ShapeDtypeStruct((BATCH, WINDOW, 128), F32),
                 jax.ShapeDtypeStruct((BATCH, CONV_WIDTH - 1, CONV_CH), F32),
                 jax.ShapeDtypeStruct((BATCH, POOL_MAX - 1, POOL_CH), F32)]
    scratch = [pltpu.VMEM((RET_HEADS, RET_HEAD_DIM, RET_HEAD_DIM), F32),
               pltpu.VMEM((TILE, 128), F32), pltpu.VMEM((TILE, 128), F32),
               pltpu.VMEM((CONV_HALO + TILE, CONV_CH), F32),
               pltpu.VMEM((POOL_HALO + TILE, POOL_CH), F32),
               pltpu.VMEM((CONV_CH, GROUP_WIDTH), BF16),
               pltpu.VMEM((4, POOL_GROUP, POOL_GROUP), BF16)]
    return pl.pallas_call(
        _mixer_prompt_kernel,
        grid=(BATCH, TILES_PER_SEQ),
        in_specs=in_specs, out_specs=out_specs, out_shape=out_shape, scratch_shapes=scratch,
        compiler_params=_params(("arbitrary", "arbitrary")),
        name="mixer_prompt",
    )(h, dmask, qdec, kdec, cdec, gn.reshape(1, -1), qg.reshape(1, -1), kg.reshape(1, -1), sinkb, biasc, biasp,
      cw, cb.reshape(1, -1), lng.reshape(1, -1), lnb.reshape(1, -1), pw2, poolw, pscale.reshape(1, -1))


def _mixer_sample_kernel(row_ref, hs_ref, s0_ref, ck_ref, cv_ref, cst_ref, pst_ref, gdec_ref, slope_ref,
                         gn_ref, qg_ref, kg_ref, sink_ref,
                         cw_ref, cb_ref, lng_ref, lnb_ref, pw2_ref, poolw_ref, pscale_ref,
                         ra_ref, mix_ref, ret_ref, kwin_ref, vwin_ref, convs_ref, pools_ref):
    b = pl.program_id(0)
    nb = pl.num_programs(0)

    for hh in range(RET_HEADS):
        lo = hh * RET_HEAD_DIM
        hi = lo + RET_HEAD_DIM
        q = row_ref[:, COL_RQ + lo:COL_RQ + hi]
        k = row_ref[:, COL_RK + lo:COL_RK + hi] * (RET_HEAD_DIM ** -0.5)
        v = row_ref[:, COL_RV + lo:COL_RV + hi]
        gate = row_ref[:, COL_RG + lo:COL_RG + hi]
        gdec = gdec_ref[hh]
        s_old = s0_ref[hh]
        q8 = jnp.broadcast_to(q, (8, RET_HEAD_DIM)).astype(BF16)
        qs = _dot(q8, s_old.astype(BF16))[0:1, :]
        a = jnp.sum(q * k, axis=-1, keepdims=True)
        o = a * v + qs * gdec
        k8 = jnp.where(lax.broadcasted_iota(jnp.int32, (8, RET_HEAD_DIM), 0) == 0, k, 0.0).astype(BF16)
        v8 = jnp.broadcast_to(v, (8, RET_HEAD_DIM)).astype(BF16)
        ret_ref[hh] = s_old * gdec + _dot_tn(k8, v8)
        y = _center_norm(o) * gn_ref[:, lo:hi]
        ra_ref[:, lo:hi] = _silu(gate) * y

    knew = row_ref[:, COL_AK:COL_AK + 128]
    vnew = row_ref[:, COL_AV:COL_AV + 128]
    scale = ATT_HEAD_DIM ** -0.5
    dist = (WINDOW - lax.broadcasted_iota(jnp.int32, (1, WINDOW), 1)).astype(F32)
    kn_parts = []
    for g in range(ATT_KV_HEADS):
        sl = slice(g * ATT_HEAD_DIM, (g + 1) * ATT_HEAD_DIM)
        kng = _rms(knew[:, sl], kg_ref[...])
        kn_parts.append(kng)
        vng = vnew[:, sl]
        q0 = COL_AQ + g * ATT_GROUP * ATT_HEAD_DIM
        qs8 = jnp.concatenate(
            [row_ref[:, q0 + i * ATT_HEAD_DIM:q0 + (i + 1) * ATT_HEAD_DIM] for i in range(ATT_GROUP)], axis=0)
        qn = _rms(qs8, qg_ref[...])
        slope = slope_ref[g]
        sink = sink_ref[g]
        s_past = _dot_nt(qn.astype(BF16), ck_ref[:, sl].astype(BF16)) * scale - slope * dist
        s_new = jnp.sum(qn * kng, axis=-1, keepdims=True) * scale
        mx = jnp.maximum(jnp.maximum(jnp.max(s_past, axis=-1, keepdims=True), s_new), sink)
        e_past = jnp.exp(s_past - mx)
        e_new = jnp.exp(s_new - mx)
        den = jnp.sum(e_past, axis=-1, keepdims=True) + e_new + jnp.exp(sink - mx)
        inv = 1.0 / den
        o = _dot((e_past * inv).astype(BF16), cv_ref[:, sl].astype(BF16)) + (e_new * inv) * vng
        for i in range(ATT_GROUP):
            c0 = GROUP_WIDTH + (g * ATT_GROUP + i) * ATT_HEAD_DIM
            ra_ref[:, c0:c0 + ATT_HEAD_DIM] = o[i:i + 1, :]
    kwin_ref[0:WINDOW - 1, :] = ck_ref[1:WINDOW, :]
    kwin_ref[WINDOW - 1:WINDOW, :] = jnp.concatenate(kn_parts, axis=-1)
    vwin_ref[0:WINDOW - 1, :] = cv_ref[1:WINDOW, :]
    vwin_ref[WINDOW - 1:WINDOW, :] = vnew

    @pl.when(b == nb - 1)
    def _():
        u = hs_ref[:, COL_CA:COL_CA + CONV_CH] * _sigmoid(hs_ref[:, COL_CG:COL_CG + CONV_CH])
        c = cw_ref[CONV_WIDTH - 1:CONV_WIDTH, :] * u
        for j in range(CONV_WIDTH - 1):
            c = c + cw_ref[j:j + 1, :] * cst_ref[j]
        c = c + cb_ref[...]
        c = _silu(_center_norm(c) * lng_ref[...] + lnb_ref[...])
        mix_ref[:, 0:GROUP_WIDTH] = _dot(c.astype(BF16), pw2_ref[...].astype(BF16)).astype(mix_ref.dtype)
        for j in range(CONV_WIDTH - 2):
            convs_ref[j] = cst_ref[j + 1]
        convs_ref[CONV_WIDTH - 2] = u

        pin = hs_ref[:, COL_PIN:COL_PIN + POOL_CH]
        npast = POOL_MAX - 1
        for gi, w in enumerate(POOL_WINDOWS):
            cs = slice(gi * POOL_GROUP, (gi + 1) * POOL_GROUP)
            xg = pin[:, cs]
            sm = xg
            for d in range(1, w):
                sm = sm + pst_ref[npast - d, :, cs]
            dd = sm / float(w) - xg
            y = _dot(dd.astype(BF16), poolw_ref[gi].astype(BF16)) * pscale_ref[:, cs]
            mix_ref[:, GROUP_WIDTH + gi * POOL_GROUP:GROUP_WIDTH + (gi + 1) * POOL_GROUP] = y.astype(mix_ref.dtype)
        for j in range(npast - 1):
            pools_ref[j] = pst_ref[j + 1]
        pools_ref[npast - 1] = pin


def _mixer_sample(layer, hs, state_ret, cache_k, cache_v, cst, pst, gdec, slopes,
                  gn, qg, kg, sinks, cw, cb, lng, lnb, pw2, poolw, pscale):
    nb = DEC_BATCH

    def full(shape):
        nd = len(shape)
        return pl.BlockSpec(shape, lambda b: (0,) * nd)

    in_specs = [pl.BlockSpec((None, 1, IN_WIDTH), lambda b: (b, 0, 0)),
                full((nb, IN_WIDTH)),
                pl.BlockSpec((None, None, RET_HEADS, RET_HEAD_DIM, RET_HEAD_DIM), lambda b: (layer, b, 0, 0, 0)),
                pl.BlockSpec((None, None, WINDOW, 128), lambda b: (layer, b, 0, 0)),
                pl.BlockSpec((None, None, WINDOW, 128), lambda b: (layer, b, 0, 0)),
                full((CONV_WIDTH - 1, nb, CONV_CH)), full((POOL_MAX - 1, nb, POOL_CH)),
                full((RET_HEADS, 1, RET_HEAD_DIM)), full((ATT_KV_HEADS, ATT_GROUP, 1)),
                full((1, GROUP_WIDTH)), full((1, ATT_HEAD_DIM)), full((1, ATT_HEAD_DIM)),
                full((ATT_KV_HEADS, ATT_GROUP, 1)),
                full((CONV_WIDTH, CONV_CH)), full((1, CONV_CH)), full((1, CONV_CH)), full((1, CONV_CH)),
                full((CONV_CH, GROUP_WIDTH)), full((4, POOL_GROUP, POOL_GROUP)), full((1, POOL_CH))]
    out_specs = [pl.BlockSpec((None, 1, 2 * GROUP_WIDTH), lambda b: (b, 0, 0)),
                 full((nb, 2 * GROUP_WIDTH)),
                 pl.BlockSpec((None, RET_HEADS, RET_HEAD_DIM, RET_HEAD_DIM), lambda b: (b, 0, 0, 0)),
                 pl.BlockSpec((None, WINDOW, 128), lambda b: (b, 0, 0)),
                 pl.BlockSpec((None, WINDOW, 128), lambda b: (b, 0, 0)),
                 full((CONV_WIDTH - 1, nb, CONV_CH)), full((POOL_MAX - 1, nb, POOL_CH))]
    out_shape = [jax.ShapeDtypeStruct((nb, 1, 2 * GROUP_WIDTH), F32),
                 jax.ShapeDtypeStruct((nb, 2 * GROUP_WIDTH), BF16),
                 jax.ShapeDtypeStruct((nb, RET_HEADS, RET_HEAD_DIM, RET_HEAD_DIM), F32),
                 jax.ShapeDtypeStruct((nb, WINDOW, 128), F32),
                 jax.ShapeDtypeStruct((nb, WINDOW, 128), F32),
                 jax.ShapeDtypeStruct((CONV_WIDTH - 1, nb, CONV_CH), F32),
                 jax.ShapeDtypeStruct((POOL_MAX - 1, nb, POOL_CH), F32)]
    ra, cp, *states = pl.pallas_call(
        _mixer_sample_kernel,
        grid=(nb,),
        in_specs=in_specs, out_specs=out_specs, out_shape=out_shape,
        compiler_params=_params(("arbitrary",)),
        name="mixer_sample",
    )(hs.reshape(nb, 1, IN_WIDTH), hs, state_ret, cache_k, cache_v, cst, pst, gdec, slopes,
      gn.reshape(1, -1), qg.reshape(1, -1), kg.reshape(1, -1), sinks.reshape(ATT_KV_HEADS, ATT_GROUP, 1),
      cw, cb.reshape(1, -1), lng.reshape(1, -1), lnb.reshape(1, -1), pw2, poolw, pscale.reshape(1, -1))
    mix = jnp.concatenate([ra.reshape(nb, 2 * GROUP_WIDTH).astype(BF16), cp], axis=-1)
    return (mix, *states)


def _retention_tables():
    c = RET_CHUNK
    log_g = jnp.log1p(-jnp.exp2(-5.0 - jnp.arange(RET_HEADS, dtype=F32)))
    idx = jnp.arange(c, dtype=F32)
    rel = idx[:, None] - idx[None, :]
    dmask = jnp.where(rel >= 0, jnp.exp(log_g[:, None, None] * jnp.maximum(rel, 0.0)), 0.0)
    ones = jnp.ones((1, 1, RET_HEAD_DIM), F32)
    qdec = jnp.exp(log_g[:, None] * (idx + 1.0))[:, :, None] * ones
    kdec = jnp.exp(log_g[:, None] * (c - 1.0 - idx))[:, :, None] * ones
    cdec = jnp.exp(log_g * c)[:, None, None] * jnp.ones((1, RET_HEAD_DIM, RET_HEAD_DIM), F32)
    gdec = jnp.exp(log_g)[:, None, None] * ones
    return (dmask, qdec, kdec, cdec), gdec


def kernel(x_prompt, x_sample, state_ret, cache_k_win, cache_v_win, state_conv, state_pool, state_ffn,
           norm1_g, w_in, ret_gn_g, q_norm_g, k_norm_g, att_sinks, conv_dw_w, conv_dw_b, conv_ln_g, conv_ln_b,
           conv_pw2, pool_w, pool_scale, w_out, norm2_g, ffn_up, ffn_dw_w, ffn_dw_b, ffn_down):
    tables, gdec = _retention_tables()
    bias = _attention_bias_tables()
    slopes =jnp.exp2(-8.0 * jnp.arange(1, ATT_HEADS + 1, dtype=F32) / ATT_HEADS).reshape(ATT_KV_HEADS, ATT_GROUP, 1)
    cache_k = cache_k_win.reshape(DEPTH, DEC_BATCH, WINDOW, 128)
    cache_v = cache_v_win.reshape(DEPTH, DEC_BATCH, WINDOW, 128)

    xp = x_prompt.reshape(NTOK, D_MODEL)
    xs = x_sample.reshape(DEC_BATCH, D_MODEL)
    outs_p = [[] for _ in range(6)]
    outs_s = [[] for _ in range(6)]
    for l in range(DEPTH):
        xnp = _rmsnorm(xp, norm1_g[l], 256)
        xns = _rmsnorm(xs, norm1_g[l], DEC_BATCH)
        hp, hs = _matmul(xnp, xns, w_in, l, 1024, 768, name="w_in")
        mixp, ret_p, kwin_p, vwin_p, conv_p, pool_p = _mixer_prompt(
            hp, tables, bias, ret_gn_g[l], q_norm_g[l], k_norm_g[l], att_sinks[l], conv_dw_w[l], conv_dw_b[l],
            conv_ln_g[l], conv_ln_b[l], conv_pw2[l], pool_w[l], pool_scale[l])
        cst = jnp.swapaxes(state_conv[l], 0, 1)
        pst = jnp.swapaxes(state_pool[l], 0, 1)
        mixs, ret_s, kwin_s, vwin_s, conv_s, pool_s = _mixer_sample(
            l, hs, state_ret, cache_k, cache_v, cst, pst, gdec, slopes,
            ret_gn_g[l], q_norm_g[l], k_norm_g[l], att_sinks[l], conv_dw_w[l], conv_dw_b[l],
            conv_ln_g[l], conv_ln_b[l], conv_pw2[l], pool_w[l], pool_scale[l])
        xp, xs = _matmul(mixp, mixs, w_out, l, 1024, 512, res=(xp, xs), name="w_out")
        xnp = _rmsnorm(xp, norm2_g[l], 256)
        xns = _rmsnorm(xs, norm2_g[l], DEC_BATCH)
        fst = jnp.swapaxes(state_ffn[l], 0, 1)
        xp, xs, ffn_p, ffn_s = _mlp(xnp, xns, xp, xs, ffn_up, ffn_down, l, ffn_dw_w[l], ffn_dw_b[l], fst)
        for lst, v in zip(outs_p, (ret_p, kwin_p.reshape(BATCH, WINDOW, ATT_KV_HEADS, ATT_HEAD_DIM),
                                   vwin_p.reshape(BATCH, WINDOW, ATT_KV_HEADS, ATT_HEAD_DIM), conv_p, pool_p, ffn_p)):
            lst.append(v)
        for lst, v in zip(outs_s, (ret_s, kwin_s.reshape(DEC_BATCH, WINDOW, ATT_KV_HEADS, ATT_HEAD_DIM),
                                   vwin_s.reshape(DEC_BATCH, WINDOW, ATT_KV_HEADS, ATT_HEAD_DIM),
                                   jnp.swapaxes(conv_s, 0, 1), jnp.swapaxes(pool_s, 0, 1),
                                   jnp.swapaxes(ffn_s, 0, 1))):
            lst.append(v)
    y_prompt = xp.reshape(BATCH, SEQ, D_MODEL)
    y_sample = xs.reshape(DEC_BATCH, 1, D_MODEL)
    return (y_prompt, y_sample, *[jnp.stack(v) for v in outs_p], *[jnp.stack(v) for v in outs_s])
```

```python
import functools

import jax
import jax.numpy as jnp
from jax import lax
from jax.experimental import pallas as pl
from jax.experimental.pallas import tpu as pltpu

D_MODEL = 4096
BATCH = 4
SEQ = 2048
DEPTH = 4
DEC_BATCH = 32
PAST_LEN = 8192
GROUP_WIDTH = 1024
RET_HEADS = 8
RET_HEAD_DIM = 128
RET_CHUNK = 128
ATT_HEAD_DIM = 64
ATT_HEADS = 16
ATT_KV_HEADS = 2
ATT_GROUP = 8
WINDOW = 128
CONV_CH = 1024
CONV_WIDTH = 31
POOL_CH = 1024
POOL_WINDOWS = (2, 4, 8, 16)
POOL_GROUP = 256
POOL_MAX = 16
D_FF = 11008
FFN_CONV_WIDTH = 3
NORM_EPS = 1e-6
IN_WIDTH = 8448

COL_RQ, COL_RK, COL_RV, COL_RG = 0, 1024, 2048, 3072
COL_AQ, COL_AK, COL_AV = 4096, 5120, 5248
COL_CA, COL_CG, COL_PIN = 5376, 6400, 7424

NTOK = BATCH * SEQ
TILE = 128
TILES_PER_SEQ = SEQ // TILE
CONV_HALO = 32
POOL_HALO = 16
FFN_HALO = 8

BF16 = jnp.bfloat16
F32 = jnp.float32
VMEM_LIMIT = 62 * 1024 * 1024


def _dot(a, b):
    return jnp.dot(a, b, preferred_element_type=F32)


def _dot_nt(a, b):
    return lax.dot_general(a, b, (((1,), (1,)), ((), ())), preferred_element_type=F32)


def _dot_tn(a, b):
    return lax.dot_general(a, b, (((0,), (0,)), ((), ())), preferred_element_type=F32)


def _sigmoid(x):
    return 1.0 / (1.0 + jnp.exp(-x))


def _silu(x):
    return x * _sigmoid(x)


def _rms(x, g):
    return x * lax.rsqrt(jnp.mean(x * x, axis=-1, keepdims=True) + NORM_EPS) * g


def _center_norm(x):
    mu = jnp.mean(x, axis=-1, keepdims=True)
    xc = x - mu
    return xc * lax.rsqrt(jnp.mean(xc * xc, axis=-1, keepdims=True) + NORM_EPS)


def _params(sem):
    return pltpu.CompilerParams(dimension_semantics=sem, vmem_limit_bytes=VMEM_LIMIT)


def _rmsnorm_kernel(x_ref, g_ref, o_ref):
    o_ref[...] = _rms(x_ref[...], g_ref[...]).astype(o_ref.dtype)


def _rmsnorm(x, g, rows):
    m, d = x.shape
    return pl.pallas_call(
        _rmsnorm_kernel,
        grid=(m // rows,),
        in_specs=[pl.BlockSpec((rows, d), lambda i: (i, 0)),
                  pl.BlockSpec((1, d), lambda i: (0, 0))],
        out_specs=pl.BlockSpec((rows, d), lambda i: (i, 0)),
        out_shape=jax.ShapeDtypeStruct((m, d), BF16),
        compiler_params=_params(("arbitrary",)),
        name="rmsnorm",
    )(x, g.reshape(1, d))


def _mm_kernel(*refs, has_res):
    if has_res:
        xp_ref, xs_ref, w_ref, rp_ref, rs_ref, op_ref, os_ref, wbf_ref = refs
    else:
        xp_ref, xs_ref, w_ref, op_ref, os_ref, wbf_ref = refs

    @pl.when(pl.program_id(1) == 0)
    def _():
        wbf_ref[...] = w_ref[...].astype(BF16)
        acc = _dot(xs_ref[...], wbf_ref[...])
        if has_res:
            acc = acc + rs_ref[...]
        os_ref[...] = acc.astype(os_ref.dtype)

    acc = _dot(xp_ref[...], wbf_ref[...])
    if has_res:
        acc = acc + rp_ref[...]
    op_ref[...] = acc.astype(op_ref.dtype)


def _matmul(xp, xs, w, layer, bm, bn, res=None, name="matmul"):
    m, k = xp.shape
    s = xs.shape[0]
    n = w.shape[2]
    in_specs = [pl.BlockSpec((bm, k), lambda j, i: (i, 0)),
                pl.BlockSpec((s, k), lambda j, i: (0, 0)),
                pl.BlockSpec((None, k, bn), lambda j, i: (layer, 0, j))]
    args = [xp, xs, w]
    if res is not None:
        in_specs += [pl.BlockSpec((bm, bn), lambda j, i: (i, j)),
                     pl.BlockSpec((s, bn), lambda j, i: (0, j))]
        args += list(res)
    return pl.pallas_call(
        functools.partial(_mm_kernel, has_res=res is not None),
        grid=(pl.cdiv(n, bn), m // bm),
        in_specs=in_specs,
        out_specs=[pl.BlockSpec((bm, bn), lambda j, i: (i, j)),
                   pl.BlockSpec((s, bn), lambda j, i: (0, j))],
        out_shape=[jax.ShapeDtypeStruct((m, n), F32), jax.ShapeDtypeStruct((s, n), F32)],
        scratch_shapes=[pltpu.VMEM((k, bn), BF16)],
        compiler_params=_params(("arbitrary", "arbitrary")),
        name=name,
    )(*args)


MLP_BM = 1024
MLP_BF = 256
MLP_NF = D_FF // MLP_BF
MLP_RES = 256
MLP_DOWN_CHUNK = 512
MLP_PARTS = 16


def _mlp_kernel(xn_ref, xns_ref, xres_ref, xsres_ref, wg_ref, wv_ref, wd_ref, dw_ref, db_ref, past_ref,
                out_ref, outs_ref, ffnp_ref, ffns_ref,
                g_ref, v_ref, act_ref, carry_ref, gs_ref, vs_ref, acts_ref):
    m = pl.program_id(0)
    s = pl.program_id(1)
    bm = MLP_BM
    w0 = dw_ref[0:1, :]
    w1 = dw_ref[1:2, :]
    w2 = dw_ref[2:3, :]
    bias = db_ref[...]
    cslot = jnp.where(s == 0, MLP_NF, s - 1)

    @pl.when(s == 0)
    def _():
        out_ref[...] = jnp.zeros(out_ref.shape, F32)
        g_ref[...] = jnp.zeros(g_ref.shape, F32)
        v_ref[...] = jnp.zeros(v_ref.shape, F32)
        act_ref[...] = jnp.zeros(act_ref.shape, BF16)

    @pl.when((s == 0) & (m == 0))
    def _():
        outs_ref[...] = xsres_ref[...]
        gs_ref[...] = jnp.zeros(gs_ref.shape, F32)
        vs_ref[...] = jnp.zeros(vs_ref.shape, F32)
        acts_ref[...] = jnp.zeros(acts_ref.shape, BF16)

    for c in range(D_MODEL // MLP_RES):
        @pl.when(s == c)
        def _(c=c):
            out_ref[:, c * MLP_RES:(c + 1) * MLP_RES] += xres_ref[...]

    def down(act, acc_ref):
        for nc in range(D_MODEL // MLP_DOWN_CHUNK):
            cs = slice(nc * MLP_DOWN_CHUNK, (nc + 1) * MLP_DOWN_CHUNK)
            acc_ref[:, cs] += _dot(act, wd_ref[:, cs].astype(BF16))

    def up(x_ref):
        g = None
        v = None
        for kc in range(4):
            ks = slice(kc * 1024, (kc + 1) * 1024)
            x = x_ref[:, ks]
            pg = _dot(x, wg_ref[ks, :].astype(BF16))
            pv = _dot(x, wv_ref[ks, :].astype(BF16))
            g = pg if g is None else g + pg
            v = pv if v is None else v + pv
        return g, v

    wslot = s % 2
    rslot = 1 - wslot

    def sample_step(do_up, do_epilogue):
        down(acts_ref[rslot], outs_ref)
        if do_epilogue:
            gs = gs_ref[...]
            c = w0 * past_ref[0] + w1 * past_ref[1] + w2 * gs + bias
            acts_ref[wslot] = (_silu(c) * vs_ref[...]).astype(BF16)
            ffns_ref[0] = past_ref[1]
            ffns_ref[1] = gs
        if do_up:
            g_new, v_new = up(xns_ref)
            gs_ref[...] = g_new
            vs_ref[...] = v_new

    def prompt_step(do_up, do_epilogue):
        if do_epilogue:
            @pl.when(m % (SEQ // bm) == 0)
            def _():
                g_ref[0:FFN_HALO, :] = jnp.zeros((FFN_HALO, MLP_BF), F32)

            @pl.when(m % (SEQ // bm) != 0)
            def _():
                g_ref[0:FFN_HALO, :] = carry_ref[cslot]

        if do_up and do_epilogue:
            parts = MLP_PARTS
            rows = bm // parts
            kw = D_MODEL // parts
            g_new = v_new = None
            tie = None
            for q in range(parts):
                x = xn_ref[:, q * kw:(q + 1) * kw]
                if tie is None:
                    wg = wg_ref[q * kw:(q + 1) * kw, :].astype(BF16)
                else:
                    wg = jnp.concatenate([(wg_ref[q * kw:q * kw + 16, :] + tie).astype(BF16),
                                          wg_ref[q * kw + 16:(q + 1) * kw, :].astype(BF16)], axis=0)
                pg = _dot(x, wg)
                pv = _dot(x, wv_ref[q * kw:(q + 1) * kw, :].astype(BF16))
                g_new = pg if g_new is None else g_new + pg
                v_new = pv if v_new is None else v_new + pv
                lo = FFN_HALO + q * rows
                g = g_ref[lo:lo + rows, :]
                g1 = g_ref[lo - 1:lo - 1 + rows, :]
                g2 = g_ref[lo - 2:lo - 2 + rows, :]
                c = w0 * g2 + w1 * g1 + w2 * g + bias
                a = _silu(c) * v_ref[q * rows:(q + 1) * rows, :]
                act_ref[wslot, q * rows:(q + 1) * rows, :] = a.astype(BF16)
                t = a[:, 0:128] + a[:, 128:256]
                t = jnp.sum(t.reshape(rows // 8, 8, 128), axis=0)
                tie = jnp.tile(t * 0.0, (2, 2))
            ffnp_ref[...] = g_ref[FFN_HALO + bm - 2:FFN_HALO + bm, :]
            carry_ref[cslot] = g_ref[bm:bm + FFN_HALO, :]
            down(act_ref[rslot], out_ref)
            g_ref[FFN_HALO:FFN_HALO + bm, :] = g_new
            v_ref[...] = v_new
            return
        if do_up:
            g_new, v_new = up(xn_ref)
        down(act_ref[rslot], out_ref)
        if do_epilogue:
            g = g_ref[FFN_HALO:FFN_HALO + bm, :]
            g1 = g_ref[FFN_HALO - 1:FFN_HALO - 1 + bm, :]
            g2 = g_ref[FFN_HALO - 2:FFN_HALO - 2 + bm, :]
            c = w0 * g2 + w1 * g1 + w2 * g + bias
            act_ref[wslot] = (_silu(c) * v_ref[...]).astype(BF16)
            ffnp_ref[...] = g_ref[FFN_HALO + bm - 2:FFN_HALO + bm, :]
            carry_ref[cslot] = g_ref[bm:bm + FFN_HALO, :]
        if do_up:
            g_ref[FFN_HALO:FFN_HALO + bm, :] = g_new
            v_ref[...] = v_new

    for cond, do_up, do_epilogue in ((s < MLP_NF, True, True), (s == MLP_NF, False, True),
                                     (s == MLP_NF + 1, False, False)):
        @pl.when(cond & (m == 0))
        def _(do_up=do_up, do_epilogue=do_epilogue):
            sample_step(do_up, do_epilogue)

        @pl.when(cond)
        def _(do_up=do_up, do_epilogue=do_epilogue):
            prompt_step(do_up, do_epilogue)


def _mlp(xnp, xns, xp, xs, w_up, w_down, layer, dw_w, dw_b, past_t):
    m, d = xnp.shape
    sb = xns.shape[0]
    bm, bf, nf = MLP_BM, MLP_BF, MLP_NF
    tiles_per_seq = SEQ // bm
    nres = d // MLP_RES
    one = pl.Buffered(1)

    def prev(s):
        return jnp.clip(s - 1, 0, nf - 1)

    def prev2(s):
        return jnp.clip(s - 2, 0, nf - 1)

    in_specs = [pl.BlockSpec((bm, d), lambda i, s: (i, 0), pipeline_mode=one),
                pl.BlockSpec((sb, d), lambda i, s: (0, 0)),
                pl.BlockSpec((bm, MLP_RES), lambda i, s: (i, jnp.minimum(s, nres - 1))),
                pl.BlockSpec((sb, d), lambda i, s: (0, 0)),
                pl.BlockSpec((None, d, bf), lambda i, s: (layer, 0, jnp.minimum(s, nf - 1))),
                pl.BlockSpec((None, d, bf), lambda i, s: (layer, 0, jnp.minimum(s, nf - 1) + nf)),
                pl.BlockSpec((None, bf, d), lambda i, s: (layer, prev2(s), 0)),
                pl.BlockSpec((FFN_CONV_WIDTH, bf), lambda i, s: (0, prev(s))),
                pl.BlockSpec((1, bf), lambda i, s: (0, prev(s))),
                pl.BlockSpec((2, sb, bf), lambda i, s: (0, 0, prev(s)))]
    out_specs = [pl.BlockSpec((bm, d), lambda i, s: (i, 0), pipeline_mode=one),
                 pl.BlockSpec((sb, d), lambda i, s: (0, 0)),
                 pl.BlockSpec((None, 2, bf), lambda i, s: (i, 0, prev(s))),
                 pl.BlockSpec((2, sb, bf), lambda i, s: (0, 0, jnp.where(i == 0, prev(s), nf - 1)))]
    out_shape = [jax.ShapeDtypeStruct((m, d), F32),
                 jax.ShapeDtypeStruct((sb, d), F32),
                 jax.ShapeDtypeStruct((m // bm, 2, D_FF), F32),
                 jax.ShapeDtypeStruct((2, sb, D_FF), F32)]
    scratch = [pltpu.VMEM((FFN_HALO + bm, bf), F32), pltpu.VMEM((bm, bf), F32), pltpu.VMEM((2, bm, bf), BF16),
               pltpu.VMEM((nf + 1, FFN_HALO, bf), F32),
               pltpu.VMEM((sb, bf), F32), pltpu.VMEM((sb, bf), F32), pltpu.VMEM((2, sb, bf), BF16)]
    yp, ys, tails, ffn_s = pl.pallas_call(
        _mlp_kernel,
        grid=(m // bm, nf + 2),
        in_specs=in_specs, out_specs=out_specs, out_shape=out_shape, scratch_shapes=scratch,
        compiler_params=_params(("arbitrary", "arbitrary")),
        name="conv_ffn",
    )(xnp, xns, xp, xs, w_up, w_up, w_down, dw_w, dw_b.reshape(1, D_FF), past_t)
    return yp, ys, tails[tiles_per_seq - 1::tiles_per_seq], ffn_s


def _mixer_prompt_kernel(h_ref, dmask_ref, qdec_ref, kdec_ref, cdec_ref, gn_ref, qg_ref, kg_ref,
                         sinkb_ref, biasc_ref, biasp_ref,
                         cw_ref, cb_ref, lng_ref, lnb_ref, pw2_ref, poolw_ref, pscale_ref,
                         mix_ref, ret_ref, kwin_ref, vwin_ref, convp_ref, poolp_ref,
                         s_ref, kprev_ref, vprev_ref, uwin_ref, pwin_ref, pw2bf_ref, poolwbf_ref):
    b = pl.program_id(0)
    t = pl.program_id(1)

    @pl.when((b == 0) & (t == 0))
    def _():
        pw2bf_ref[...] = pw2_ref[...].astype(BF16)
        poolwbf_ref[...] = poolw_ref[...].astype(BF16)

    @pl.when(t == 0)
    def _():
        s_ref[...] = jnp.zeros(s_ref.shape, F32)
        kprev_ref[...] = jnp.zeros(kprev_ref.shape, F32)
        vprev_ref[...] = jnp.zeros(vprev_ref.shape, F32)
        uwin_ref[0:CONV_HALO, :] = jnp.zeros((CONV_HALO, CONV_CH), F32)
        pwin_ref[0:POOL_HALO, :] = jnp.zeros((POOL_HALO, POOL_CH), F32)

    ret_out, ret_state = [], []
    for hh in range(RET_HEADS):
        lo = hh * RET_HEAD_DIM
        hi = lo + RET_HEAD_DIM
        q = h_ref[:, COL_RQ + lo:COL_RQ + hi]
        k = h_ref[:, COL_RK + lo:COL_RK + hi] * (RET_HEAD_DIM ** -0.5)
        v = h_ref[:, COL_RV + lo:COL_RV + hi]
        qb = q.astype(BF16)
        vb = v.astype(BF16)
        a = _dot_nt(qb, k.astype(BF16)) * dmask_ref[hh]
        s_old = s_ref[hh]
        o = _dot(a.astype(BF16), vb) + _dot(qb, s_old.astype(BF16)) * qdec_ref[hh]
        ret_state.append(s_old * cdec_ref[hh] + _dot_tn((k * kdec_ref[hh]).astype(BF16), vb))
        y = _center_norm(o) * gn_ref[:, lo:hi]
        gate = h_ref[:, COL_RG + lo:COL_RG + hi]
        ret_out.append((_silu(gate) * y).astype(mix_ref.dtype))
    mix_ref[:, 0:GROUP_WIDTH] = jnp.concatenate(ret_out, axis=-1)
    for hh in range(RET_HEADS):
        s_ref[hh] = ret_state[hh]
        ret_ref[hh] = ret_state[hh]

    lane = lax.broadcasted_iota(jnp.int32, (1, 128), 1)
    half_mask = (lane < ATT_HEAD_DIM, lane >= ATT_HEAD_DIM)

    def pair_rms(x, gain):
        sq = x * x
        ms = [jnp.sum(jnp.where(hm, sq, 0.0), axis=-1, keepdims=True) * (1.0 / ATT_HEAD_DIM) for hm in half_mask]
        r = jnp.where(half_mask[0], lax.rsqrt(ms[0] + NORM_EPS), lax.rsqrt(ms[1] + NORM_EPS))
        return x * r * gain

    kn = pair_rms(h_ref[:, COL_AK:COL_AK + 128], kg_ref[...])
    vc = h_ref[:, COL_AV:COL_AV + 128]
    kwin_ref[...] = kn
    vwin_ref[...] = vc
    kp = kprev_ref[...]
    vp = vprev_ref[...]
    kprev_ref[...] = kn
    vprev_ref[...] = vc
    swapped = {id(x): pltpu.roll(x, ATT_HEAD_DIM, axis=1) for x in (kn, vc, kp, vp)}

    def half_of(x, src, dst):
        y = x if src == dst else swapped[id(x)]
        return jnp.where(half_mask[dst], y, 0.0).astype(BF16)

    scale = ATT_HEAD_DIM ** -0.5
    att_out = []
    for g in range(ATT_KV_HEADS):
        q0 = COL_AQ + g * ATT_GROUP * ATT_HEAD_DIM
        qst = jnp.concatenate([h_ref[:, q0 + p * 128:q0 + (p + 1) * 128] for p in range(ATT_GROUP // 2)], axis=0)
        qn = pair_rms(qst, qg_ref[...]).astype(BF16)
        o = None
        for par in range(2):
            s_cur = _dot_nt(qn, half_of(kn, g, par)) * scale + biasc_ref[g, par]
            s_prev = _dot_nt(qn, half_of(kp, g, par)) * scale + biasp_ref[g, par]
            s_prev = jnp.where(t > 0, s_prev, -jnp.inf)
            sink = sinkb_ref[g, par]
            mx = jnp.max(jnp.maximum(jnp.maximum(s_cur, s_prev), sink), axis=-1, keepdims=True)
            e_cur = jnp.exp(s_cur - mx)
            e_prev = jnp.exp(s_prev - mx)
            den = jnp.sum(e_cur + e_prev, axis=-1, keepdims=True) + jnp.exp(sink[:, 0:1] - mx)
            inv = 1.0 / den
            od = (_dot((e_cur * inv).astype(BF16), half_of(vc, g, par))
                  + _dot((e_prev * inv).astype(BF16), half_of(vp, g, par)))
            o = od if o is None else o + od
        att_out += [o[p * 128:(p + 1) * 128, :].astype(mix_ref.dtype) for p in range(ATT_GROUP // 2)]
    mix_ref[:, GROUP_WIDTH:2 * GROUP_WIDTH] = jnp.concatenate(att_out, axis=-1)

    u = h_ref[:, COL_CA:COL_CA + CONV_CH] * _sigmoid(h_ref[:, COL_CG:COL_CG + CONV_CH])
    uwin_ref[CONV_HALO:CONV_HALO + TILE, :] = u
    wrows = TILE + 8
    cols = []
    for cc in range(CONV_CH // 128):
        cs = slice(cc * 128, (cc + 1) * 128)
        acc = None
        for k in range(8):
            part = None
            for a in range(4):
                j = CONV_WIDTH - 1 - (8 * a + k)
                if j < 0:
                    continue
                start = CONV_HALO - 8 - 8 * a
                term = cw_ref[j:j + 1, cs] * uwin_ref[start:start + wrows, cs]
                part = term if part is None else part + term
            part = part[8 - k:8 - k + TILE, :]
            acc = part if acc is None else acc + part
        cols.append(acc)
    c = jnp.concatenate(cols, axis=-1) + cb_ref[...]
    c = _silu(_center_norm(c) * lng_ref[...] + lnb_ref[...])
    mix_ref[:, 2 * GROUP_WIDTH:3 * GROUP_WIDTH] = _dot(c.astype(BF16), pw2bf_ref[...]).astype(mix_ref.dtype)
    convp_ref[...] = uwin_ref[CONV_HALO + TILE - (CONV_WIDTH - 1):CONV_HALO + TILE, :]
    uwin_ref[0:CONV_HALO, :] = uwin_ref[TILE:TILE + CONV_HALO, :]

    pin = h_ref[:, COL_PIN:COL_PIN + POOL_CH]
    pwin_ref[POOL_HALO:POOL_HALO + TILE, :] = pin
    pos = (t * TILE + lax.broadcasted_iota(jnp.int32, (TILE, 1), 0) + 1).astype(F32)
    pool_out = []
    for gi, w in enumerate(POOL_WINDOWS):
        cs = slice(gi * POOL_GROUP, (gi + 1) * POOL_GROUP)
        xg = pin[:, cs]
        sm = xg
        for d in range(1, w):
            sm = sm + pwin_ref[POOL_HALO - d:POOL_HALO - d + TILE, cs]
        cnt = jnp.minimum(float(w), pos)
        dd = sm / cnt - xg
        y = _dot(dd.astype(BF16), poolwbf_ref[gi]) * pscale_ref[:, cs]
        pool_out.append(y.astype(mix_ref.dtype))
    mix_ref[:, 3 * GROUP_WIDTH:4 * GROUP_WIDTH] = jnp.concatenate(pool_out, axis=-1)
    poolp_ref[...] = pwin_ref[POOL_HALO + TILE - (POOL_MAX - 1):POOL_HALO + TILE, :]
    pwin_ref[0:POOL_HALO, :] = pwin_ref[TILE:TILE + POOL_HALO, :]


def _full(shape):
    nd = len(shape)
    return pl.BlockSpec(shape, lambda b, t: (0,) * nd)


def _stack_heads(per_head):
    x = per_head.reshape((ATT_KV_HEADS, ATT_GROUP // 2, 2) + per_head.shape[1:])
    x = jnp.swapaxes(x, 1, 2)
    return x.reshape((ATT_KV_HEADS, 2, (ATT_GROUP // 2) * per_head.shape[1]) + per_head.shape[2:])


def _attention_bias_tables():
    slopes = jnp.exp2(-8.0 * jnp.arange(1, ATT_HEADS + 1, dtype=F32) / ATT_HEADS)[:, None, None]
    idx = jnp.arange(TILE)
    rel = (idx[:, None] - idx[None, :])
    dist = rel.astype(F32)[None]
    cur = jnp.where((rel >= 0)[None], -slopes * dist, -jnp.inf)
    prev = jnp.where((rel <= 0)[None], -slopes * (dist + float(TILE)), -jnp.inf)
    return _stack_heads(cur), _stack_heads(prev)


def _mixer_prompt(h, tables, bias, gn, qg, kg, sinks, cw, cb, lng, lnb, pw2, poolw, pscale):
    dmask, qdec, kdec, cdec = tables
    biasc, biasp = bias
    sinkb = _stack_heads(jnp.broadcast_to(sinks[:, None, None], (ATT_HEADS, TILE, 128)))
    qg = jnp.tile(qg, 2)
    kg = jnp.tile(kg, 2)
    in_specs = [pl.BlockSpec((TILE, IN_WIDTH), lambda b, t: (b * TILES_PER_SEQ + t, 0)),
                _full(dmask.shape), _full(qdec.shape), _full(kdec.shape), _full(cdec.shape),
                _full((1, GROUP_WIDTH)), _full((1, 128)), _full((1, 128)),
                _full(sinkb.shape), _full(biasc.shape), _full(biasp.shape),
                _full((CONV_WIDTH, CONV_CH)), _full((1, CONV_CH)), _full((1, CONV_CH)), _full((1, CONV_CH)),
                _full((CONV_CH, GROUP_WIDTH)), _full((4, POOL_GROUP, POOL_GROUP)), _full((1, POOL_CH))]
    out_specs = [pl.BlockSpec((TILE, 4 * GROUP_WIDTH), lambda b, t: (b * TILES_PER_SEQ + t, 0)),
                 pl.BlockSpec((None, RET_HEADS, RET_HEAD_DIM, RET_HEAD_DIM), lambda b, t: (b, 0, 0, 0)),
                 pl.BlockSpec((None, WINDOW, 128), lambda b, t: (b, 0, 0)),
                 pl.BlockSpec((None, WINDOW, 128), lambda b, t: (b, 0, 0)),
                 pl.BlockSpec((None, CONV_WIDTH - 1, CONV_CH), lambda b, t: (b, 0, 0)),
                 pl.BlockSpec((None, POOL_MAX - 1, POOL_CH), lambda b, t: (b, 0, 0))]
    out_shape = [jax.ShapeDtypeStruct((NTOK, 4 * GROUP_WIDTH), BF16),
                 jax.ShapeDtypeStruct((BATCH, RET_HEADS, RET_HEAD_DIM, RET_HEAD_DIM), F32),
                 jax.ShapeDtypeStruct((BATCH, WINDOW, 128), F32),
                 jax.ShapeDtypeStruct((BATCH, WINDOW, 128), F32),
                 jax.ShapeDtypeStruct((BATCH, CONV_WIDTH - 1, CONV_CH), F32),
                 jax.ShapeDtypeStruct((BATCH, POOL_MAX - 1, POOL_CH), F32)]
    scratch = [pltpu.VMEM((RET_HEADS, RET_HEAD_DIM, RET_HEAD_DIM), F32),
               pltpu.VMEM((TILE, 128), F32), pltpu.VMEM((TILE, 128), F32),
               pltpu.VMEM((CONV_HALO + TILE, CONV_CH), F32),
               pltpu.VMEM((POOL_HALO + TILE, POOL_CH), F32),
               pltpu.VMEM((CONV_CH, GROUP_WIDTH), BF16),
               pltpu.VMEM((4, POOL_GROUP, POOL_GROUP), BF16)]
    return pl.pallas_call(
        _mixer_prompt_kernel,
        grid=(BATCH, TILES_PER_SEQ),
        in_specs=in_specs, out_specs=out_specs, out_shape=out_shape, scratch_shapes=scratch,
        compiler_params=_params(("arbitrary", "arbitrary")),
        name="mixer_prompt",
    )(h, dmask, qdec, kdec, cdec, gn.reshape(1, -1), qg.reshape(1, -1), kg.reshape(1, -1), sinkb, biasc, biasp,
      cw, cb.reshape(1, -1), lng.reshape(1, -1), lnb.reshape(1, -1), pw2, poolw, pscale.reshape(1, -1))


def _mixer_sample_kernel(row_ref, hs_ref, s0_ref, ck_ref, cv_ref, cst_ref, pst_ref, gdec_ref, slope_ref,
                         gn_ref, qg_ref, kg_ref, sink_ref,
                         cw_ref, cb_ref, lng_ref, lnb_ref, pw2_ref, poolw_ref, pscale_ref,
                         ra_ref, mix_ref, ret_ref, kwin_ref, vwin_ref, convs_ref, pools_ref):
    b = pl.program_id(0)
    nb = pl.num_programs(0)

    for hh in range(RET_HEADS):
        lo = hh * RET_HEAD_DIM
        hi = lo + RET_HEAD_DIM
        q = row_ref[:, COL_RQ + lo:COL_RQ + hi]
        k = row_ref[:, COL_RK + lo:COL_RK + hi] * (RET_HEAD_DIM ** -0.5)
        v = row_ref[:, COL_RV + lo:COL_RV + hi]
        gate = row_ref[:, COL_RG + lo:COL_RG + hi]
        gdec = gdec_ref[hh]
        s_old = s0_ref[hh]
        q8 = jnp.broadcast_to(q, (8, RET_HEAD_DIM)).astype(BF16)
        qs = _dot(q8, s_old.astype(BF16))[0:1, :]
        a = jnp.sum(q * k, axis=-1, keepdims=True)
        o = a * v + qs * gdec
        k8 = jnp.where(lax.broadcasted_iota(jnp.int32, (8, RET_HEAD_DIM), 0) == 0, k, 0.0).astype(BF16)
        v8 = jnp.broadcast_to(v, (8, RET_HEAD_DIM)).astype(BF16)
        ret_ref[hh] = s_old * gdec + _dot_tn(k8, v8)
        y = _center_norm(o) * gn_ref[:, lo:hi]
        ra_ref[:, lo:hi] = _silu(gate) * y

    knew = row_ref[:, COL_AK:COL_AK + 128]
    vnew = row_ref[:, COL_AV:COL_AV + 128]
    scale = ATT_HEAD_DIM ** -0.5
    dist = (WINDOW - lax.broadcasted_iota(jnp.int32, (1, WINDOW), 1)).astype(F32)
    kn_parts = []
    for g in range(ATT_KV_HEADS):
        sl = slice(g * ATT_HEAD_DIM, (g + 1) * ATT_HEAD_DIM)
        kng = _rms(knew[:, sl], kg_ref[...])
        kn_parts.append(kng)
        vng = vnew[:, sl]
        q0 = COL_AQ + g * ATT_GROUP * ATT_HEAD_DIM
        qs8 = jnp.concatenate(
            [row_ref[:, q0 + i * ATT_HEAD_DIM:q0 + (i + 1) * ATT_HEAD_DIM] for i in range(ATT_GROUP)], axis=0)
        qn = _rms(qs8, qg_ref[...])
        slope = slope_ref[g]
        sink = sink_ref[g]
        s_past = _dot_nt(qn.astype(BF16), ck_ref[:, sl].astype(BF16)) * scale - slope * dist
        s_new = jnp.sum(qn * kng, axis=-1, keepdims=True) * scale
        mx = jnp.maximum(jnp.maximum(jnp.max(s_past, axis=-1, keepdims=True), s_new), sink)
        e_past = jnp.exp(s_past - mx)
        e_new = jnp.exp(s_new - mx)
        den = jnp.sum(e_past, axis=-1, keepdims=True) + e_new + jnp.exp(sink - mx)
        inv = 1.0 / den
        o = _dot((e_past * inv).astype(BF16), cv_ref[:, sl].astype(BF16)) + (e_new * inv) * vng
        for i in range(ATT_GROUP):
            c0 = GROUP_WIDTH + (g * ATT_GROUP + i) * ATT_HEAD_DIM
            ra_ref[:, c0:c0 + ATT_HEAD_DIM] = o[i:i + 1, :]
    kwin_ref[0:WINDOW - 1, :] = ck_ref[1:WINDOW, :]
    kwin_ref[WINDOW - 1:WINDOW, :] = jnp.concatenate(kn_parts, axis=-1)
    vwin_ref[0:WINDOW - 1, :] = cv_ref[1:WINDOW, :]
    vwin_ref[WINDOW - 1:WINDOW, :] = vnew

    @pl.when(b == nb - 1)
    def _():
        u = hs_ref[:, COL_CA:COL_CA + CONV_CH] * _sigmoid(hs_ref[:, COL_CG:COL_CG + CONV_CH])
        c = cw_ref[CONV_WIDTH - 1:CONV_WIDTH, :] * u
        for j in range(CONV_WIDTH - 1):
            c = c + cw_ref[j:j + 1, :] * cst_ref[j]
        c = c + cb_ref[...]
        c = _silu(_center_norm(c) * lng_ref[...] + lnb_ref[...])
        mix_ref[:, 0:GROUP_WIDTH] = _dot(c.astype(BF16), pw2_ref[...].astype(BF16)).astype(mix_ref.dtype)
        for j in range(CONV_WIDTH - 2):
            convs_ref[j] = cst_ref[j + 1]
        convs_ref[CONV_WIDTH - 2] = u

        pin = hs_ref[:, COL_PIN:COL_PIN + POOL_CH]
        npast = POOL_MAX - 1
        for gi, w in enumerate(POOL_WINDOWS):
            cs = slice(gi * POOL_GROUP, (gi + 1) * POOL_GROUP)
            xg = pin[:, cs]
            sm = xg
            for d in range(1, w):
                sm = sm + pst_ref[npast - d, :, cs]
            dd = sm / float(w) - xg
            y = _dot(dd.astype(BF16), poolw_ref[gi].astype(BF16)) * pscale_ref[:, cs]
            mix_ref[:, GROUP_WIDTH + gi * POOL_GROUP:GROUP_WIDTH + (gi + 1) * POOL_GROUP] = y.astype(mix_ref.dtype)
        for j in range(npast - 1):
            pools_ref[j] = pst_ref[j + 1]
        pools_ref[npast - 1] = pin


def _mixer_sample(layer, hs, state_ret, cache_k, cache_v, cst, pst, gdec, slopes,
                  gn, qg, kg, sinks, cw, cb, lng, lnb, pw2, poolw, pscale):
    nb = DEC_BATCH

    def full(shape):
        nd = len(shape)
        return pl.BlockSpec(shape, lambda b: (0,) * nd)

    in_specs = [pl.BlockSpec((None, 1, IN_WIDTH), lambda b: (b, 0, 0)),
                full((nb, IN_WIDTH)),
                pl.BlockSpec((None, None, RET_HEADS, RET_HEAD_DIM, RET_HEAD_DIM), lambda b: (layer, b, 0, 0, 0)),
                pl.BlockSpec((None, None, WINDOW, 128), lambda b: (layer, b, 0, 0)),
                pl.BlockSpec((None, None, WINDOW, 128), lambda b: (layer, b, 0, 0)),
                full((CONV_WIDTH - 1, nb, CONV_CH)), full((POOL_MAX - 1, nb, POOL_CH)),
                full((RET_HEADS, 1, RET_HEAD_DIM)), full((ATT_KV_HEADS, ATT_GROUP, 1)),
                full((1, GROUP_WIDTH)), full((1, ATT_HEAD_DIM)), full((1, ATT_HEAD_DIM)),
                full((ATT_KV_HEADS, ATT_GROUP, 1)),
                full((CONV_WIDTH, CONV_CH)), full((1, CONV_CH)), full((1, CONV_CH)), full((1, CONV_CH)),
                full((CONV_CH, GROUP_WIDTH)), full((4, POOL_GROUP, POOL_GROUP)), full((1, POOL_CH))]
    out_specs = [pl.BlockSpec((None, 1, 2 * GROUP_WIDTH), lambda b: (b, 0, 0)),
                 full((nb, 2 * GROUP_WIDTH)),
                 pl.BlockSpec((None, RET_HEADS, RET_HEAD_DIM, RET_HEAD_DIM), lambda b: (b, 0, 0, 0)),
                 pl.BlockSpec((None, WINDOW, 128), lambda b: (b, 0, 0)),
                 pl.BlockSpec((None, WINDOW, 128), lambda b: (b, 0, 0)),
                 full((CONV_WIDTH - 1, nb, CONV_CH)), full((POOL_MAX - 1, nb, POOL_CH))]
    out_shape = [jax.ShapeDtypeStruct((nb, 1, 2 * GROUP_WIDTH), F32),
                 jax.ShapeDtypeStruct((nb, 2 * GROUP_WIDTH), BF16),
                 jax.ShapeDtypeStruct((nb, RET_HEADS, RET_HEAD_DIM, RET_HEAD_DIM), F32),
                 jax.ShapeDtypeStruct((nb, WINDOW, 128), F32),
                 jax.ShapeDtypeStruct((nb, WINDOW, 128), F32),
                 jax.ShapeDtypeStruct((CONV_WIDTH - 1, nb, CONV_CH), F32),
                 jax.ShapeDtypeStruct((POOL_MAX - 1, nb, POOL_CH), F32)]
    ra, cp, *states = pl.pallas_call(
        _mixer_sample_kernel,
        grid=(nb,),
        in_specs=in_specs, out_specs=out_specs, out_shape=out_shape,
        compiler_params=_params(("arbitrary",)),
        name="mixer_sample",
    )(hs.reshape(nb, 1, IN_WIDTH), hs, state_ret, cache_k, cache_v, cst, pst, gdec, slopes,
      gn.reshape(1, -1), qg.reshape(1, -1), kg.reshape(1, -1), sinks.reshape(ATT_KV_HEADS, ATT_GROUP, 1),
      cw, cb.reshape(1, -1), lng.reshape(1, -1), lnb.reshape(1, -1), pw2, poolw, pscale.reshape(1, -1))
    mix = jnp.concatenate([ra.reshape(nb, 2 * GROUP_WIDTH).astype(BF16), cp], axis=-1)
    return (mix, *states)


def _retention_tables():
    c = RET_CHUNK
    log_g = jnp.log1p(-jnp.exp2(-5.0 - jnp.arange(RET_HEADS, dtype=F32)))
    idx = jnp.arange(c, dtype=F32)
    rel = idx[:, None] - idx[None, :]
    dmask = jnp.where(rel >= 0, jnp.exp(log_g[:, None, None] * jnp.maximum(rel, 0.0)), 0.0)
    ones = jnp.ones((1, 1, RET_HEAD_DIM), F32)
    qdec = jnp.exp(log_g[:, None] * (idx + 1.0))[:, :, None] * ones
    kdec = jnp.exp(log_g[:, None] * (c - 1.0 - idx))[:, :, None] * ones
    cdec = jnp.exp(log_g * c)[:, None, None] * jnp.ones((1, RET_HEAD_DIM, RET_HEAD_DIM), F32)
    gdec = jnp.exp(log_g)[:, None, None] * ones
    return (dmask, qdec, kdec, cdec), gdec


def kernel(x_prompt, x_sample, state_ret, cache_k_win, cache_v_win, state_conv, state_pool, state_ffn,
           norm1_g, w_in, ret_gn_g, q_norm_g, k_norm_g, att_sinks, conv_dw_w, conv_dw_b, conv_ln_g, conv_ln_b,
           conv_pw2, pool_w, pool_scale, w_out, norm2_g, ffn_up, ffn_dw_w, ffn_dw_b, ffn_down):
    tables, gdec = _retention_tables()
    bias = _attention_bias_tables()
    slopes =jnp.exp2(-8.0 * jnp.arange(1, ATT_HEADS + 1, dtype=F32) / ATT_HEADS).reshape(ATT_KV_HEADS, ATT_GROUP, 1)
    cache_k = cache_k_win.reshape(DEPTH, DEC_BATCH, WINDOW, 128)
    cache_v = cache_v_win.reshape(DEPTH, DEC_BATCH, WINDOW, 128)

    xp = x_prompt.reshape(NTOK, D_MODEL)
    xs = x_sample.reshape(DEC_BATCH, D_MODEL)
    outs_p = [[] for _ in range(6)]
    outs_s = [[] for _ in range(6)]
    for l in range(DEPTH):
        xnp = _rmsnorm(xp, norm1_g[l], 256)
        xns = _rmsnorm(xs, norm1_g[l], DEC_BATCH)
        hp, hs = _matmul(xnp, xns, w_in, l, 1024, 768, name="w_in")
        mixp, ret_p, kwin_p, vwin_p, conv_p, pool_p = _mixer_prompt(
            hp, tables, bias, ret_gn_g[l], q_norm_g[l], k_norm_g[l], att_sinks[l], conv_dw_w[l], conv_dw_b[l],
            conv_ln_g[l], conv_ln_b[l], conv_pw2[l], pool_w[l], pool_scale[l])
        cst = jnp.swapaxes(state_conv[l], 0, 1)
        pst = jnp.swapaxes(state_pool[l], 0, 1)
        mixs, ret_s, kwin_s, vwin_s, conv_s, pool_s = _mixer_sample(
            l, hs, state_ret, cache_k, cache_v, cst, pst, gdec, slopes,
            ret_gn_g[l], q_norm_g[l], k_norm_g[l], att_sinks[l], conv_dw_w[l], conv_dw_b[l],
            conv_ln_g[l], conv_ln_b[l], conv_pw2[l], pool_w[l], pool_scale[l])
        xp, xs = _matmul(mixp, mixs, w_out, l, 1024, 512, res=(xp, xs), name="w_out")
        xnp = _rmsnorm(xp, norm2_g[l], 256)
        xns = _rmsnorm(xs, norm2_g[l], DEC_BATCH)
        fst = jnp.swapaxes(state_ffn[l], 0, 1)
        xp, xs, ffn_p, ffn_s = _mlp(xnp, xns, xp, xs, ffn_up, ffn_down, l, ffn_dw_w[l], ffn_dw_b[l], fst)
        for lst, v in zip(outs_p, (ret_p, kwin_p.reshape(BATCH, WINDOW, ATT_KV_HEADS, ATT_HEAD_DIM),
                                   vwin_p.reshape(BATCH, WINDOW, ATT_KV_HEADS, ATT_HEAD_DIM), conv_p, pool_p, ffn_p)):
            lst.append(v)
        for lst, v in zip(outs_s, (ret_s, kwin_s.reshape(DEC_BATCH, WINDOW, ATT_KV_HEADS, ATT_HEAD_DIM),
                                   vwin_s.reshape(DEC_BATCH, WINDOW, ATT_KV_HEADS, ATT_HEAD_DIM),
                                   jnp.swapaxes(conv_s, 0, 1), jnp.swapaxes(pool_s, 0, 1),
                                   jnp.swapaxes(ffn_s, 0, 1))):
            lst.append(v)
    y_prompt = xp.reshape(BATCH, SEQ, D_MODEL)
    y_sample = xs.reshape(DEC_BATCH, 1, D_MODEL)
    return (y_prompt, y_sample, *[jnp.stack(v) for v in outs_p], *[jnp.stack(v) for v in outs_s])
```

```python
import jax
import jax.numpy as jnp
from jax import lax
from jax.experimental import pallas as pl
from jax.experimental.pallas import tpu as pltpu

D_MODEL = 4096
BATCH = 4
SEQ = 2048
DEPTH = 4
DEC_BATCH = 32
PAST_LEN = 8192
GROUP_WIDTH = 1024
RET_HEADS = 8
RET_HEAD_DIM = 128
RET_CHUNK = 128
ATT_HEAD_DIM = 64
ATT_HEADS = 16
ATT_KV_HEADS = 2
ATT_GROUP = 8
WINDOW = 128
CONV_CH = 1024
CONV_WIDTH = 31
POOL_CH = 1024
POOL_WINDOWS = (2, 4, 8, 16)
POOL_GROUP = 256
POOL_MAX = 16
D_FF = 11008
FFN_CONV_WIDTH = 3
NORM_EPS = 1e-6
IN_WIDTH = 8448

COL_RQ, COL_RK, COL_RV, COL_RG = 0, 1024, 2048, 3072
COL_AQ, COL_AK, COL_AV = 4096, 5120, 5248
COL_CA, COL_CG, COL_PIN = 5376, 6400, 7424

NTOK = BATCH * SEQ
TILE = 128
TILES_PER_SEQ = SEQ // TILE
CONV_HALO = 32
POOL_HALO = 16
FFN_HALO = 8

BF16 = jnp.bfloat16
F32 = jnp.float32
VMEM_LIMIT = 62 * 1024 * 1024


def _dot(a, b):
    return jnp.dot(a, b, preferred_element_type=F32)


def _dot_nt(a, b):
    return lax.dot_general(a, b, (((1,), (1,)), ((), ())), preferred_element_type=F32)


def _dot_tn(a, b):
    return lax.dot_general(a, b, (((0,), (0,)), ((), ())), preferred_element_type=F32)


def _sigmoid(x):
    return 1.0 / (1.0 + jnp.exp(-x))


def _silu(x):
    return x * _sigmoid(x)


def _rms(x, g):
    return x * lax.rsqrt(jnp.mean(x * x, axis=-1, keepdims=True) + NORM_EPS) * g


def _center_norm(x):
    mu = jnp.mean(x, axis=-1, keepdims=True)
    xc = x - mu
    return xc * lax.rsqrt(jnp.mean(xc * xc, axis=-1, keepdims=True) + NORM_EPS)


def _params(sem):
    return pltpu.CompilerParams(dimension_semantics=sem, vmem_limit_bytes=VMEM_LIMIT)


def _rmsnorm_kernel(x_ref, g_ref, o_ref):
    o_ref[...] = _rms(x_ref[...], g_ref[...]).astype(o_ref.dtype)


def _rmsnorm(x, g, rows):
    m, d = x.shape
    return pl.pallas_call(
        _rmsnorm_kernel,
        grid=(m // rows,),
        in_specs=[pl.BlockSpec((rows, d), lambda i: (i, 0)),
                  pl.BlockSpec((1, d), lambda i: (0, 0))],
        out_specs=pl.BlockSpec((rows, d), lambda i: (i, 0)),
        out_shape=jax.ShapeDtypeStruct((m, d), BF16),
        compiler_params=_params(("arbitrary",)),
        name="rmsnorm",
    )(x, g.reshape(1, d))


def _mm_kernel(xp_ref, xs_ref, w_ref, op_ref, os_ref, wbf_ref):
    @pl.when(pl.program_id(1) == 0)
    def _():
        wbf_ref[...] = w_ref[...].astype(BF16)
        os_ref[...] = _dot(xs_ref[...], wbf_ref[...])

    op_ref[...] = _dot(xp_ref[...], wbf_ref[...])


def _matmul(xp, xs, w, layer, bm, bn, name):
    m, k = xp.shape
    s = xs.shape[0]
    n = w.shape[2]
    return pl.pallas_call(
        _mm_kernel,
        grid=(n // bn, m // bm),
        in_specs=[pl.BlockSpec((bm, k), lambda j, i: (i, 0)),
                  pl.BlockSpec((s, k), lambda j, i: (0, 0)),
                  pl.BlockSpec((None, k, bn), lambda j, i: (layer, 0, j))],
        out_specs=[pl.BlockSpec((bm, bn), lambda j, i: (i, j)),
                   pl.BlockSpec((s, bn), lambda j, i: (0, j))],
        out_shape=[jax.ShapeDtypeStruct((m, n), F32), jax.ShapeDtypeStruct((s, n), F32)],
        scratch_shapes=[pltpu.VMEM((k, bn), BF16)],
        compiler_params=_params(("arbitrary", "arbitrary")),
        name=name,
    )(xp, xs, w)


def _wout_kernel(xp_ref, xs_ref, w_ref, rp_ref, rs_ref, g_ref,
                 op_ref, os_ref, xgp_ref, xgs_ref, rrp_ref, rrs_ref,
                 wbf_ref, ssqp_ref, ssqs_ref):
    j = pl.program_id(0)
    i = pl.program_id(1)
    bm = xp_ref.shape[0]
    gain = g_ref[...]
    rows = pl.ds(pl.multiple_of(i * bm, bm), bm)

    def row_ssq(x):
        return jnp.broadcast_to(jnp.sum(x * x, axis=-1, keepdims=True), (x.shape[0], 128))

    @pl.when(i == 0)
    def _():
        wbf_ref[...] = w_ref[...].astype(BF16)
        acc = _dot(xs_ref[...], wbf_ref[...]) + rs_ref[...]
        os_ref[...] = acc
        xgs_ref[...] = (acc * gain).astype(BF16)

        @pl.when(j == 0)
        def _():
            ssqs_ref[...] = jnp.zeros(ssqs_ref.shape, F32)

        ssqs_ref[...] += row_ssq(acc)
        rrs_ref[...] = lax.rsqrt(ssqs_ref[...] * (1.0 / D_MODEL) + NORM_EPS)

    acc = _dot(xp_ref[...], wbf_ref[...]) + rp_ref[...]
    op_ref[...] = acc
    xgp_ref[...] = (acc * gain).astype(BF16)
    part = row_ssq(acc)

    @pl.when(j == 0)
    def _():
        ssqp_ref[rows, :] = part

    @pl.when(j > 0)
    def _():
        ssqp_ref[rows, :] += part

    rrp_ref[...] = lax.rsqrt(ssqp_ref[rows, :] * (1.0 / D_MODEL) + NORM_EPS)


def _w_out(mixp, mixs, w, layer, xp, xs, gain, bm, bn):
    m, k = mixp.shape
    s = mixs.shape[0]
    n = w.shape[2]
    nj = n // bn
    in_specs = [pl.BlockSpec((bm, k), lambda j, i: (i, 0)),
                pl.BlockSpec((s, k), lambda j, i: (0, 0)),
                pl.BlockSpec((None, k, bn), lambda j, i: (layer, 0, j)),
                pl.BlockSpec((bm, bn), lambda j, i: (i, j)),
                pl.BlockSpec((s, bn), lambda j, i: (0, j)),
                pl.BlockSpec((1, bn), lambda j, i: (0, j))]
    out_specs = [pl.BlockSpec((bm, bn), lambda j, i: (i, j)),
                 pl.BlockSpec((s, bn), lambda j, i: (0, j)),
                 pl.BlockSpec((bm, bn), lambda j, i: (i, j)),
                 pl.BlockSpec((s, bn), lambda j, i: (0, j)),
                 pl.BlockSpec((bm, 128), lambda j, i: (jnp.where(j == nj - 1, i, 0), 0)),
                 pl.BlockSpec((s, 128), lambda j, i: (0, 0))]
    out_shape = [jax.ShapeDtypeStruct((m, n), F32), jax.ShapeDtypeStruct((s, n), F32),
                 jax.ShapeDtypeStruct((m, n), BF16), jax.ShapeDtypeStruct((s, n), BF16),
                 jax.ShapeDtypeStruct((m, 128), F32), jax.ShapeDtypeStruct((s, 128), F32)]
    return pl.pallas_call(
        _wout_kernel,
        grid=(nj, m // bm),
        in_specs=in_specs, out_specs=out_specs, out_shape=out_shape,
        scratch_shapes=[pltpu.VMEM((k, bn), BF16), pltpu.VMEM((m, 128), F32), pltpu.VMEM((s, 128), F32)],
        compiler_params=_params(("arbitrary", "arbitrary")),
        name="w_out",
    )(mixp, mixs, w, xp, xs, gain.reshape(1, n))


MLP_BM = 1024
MLP_BF = 256
MLP_NF = D_FF // MLP_BF
MLP_RES = 256
MLP_DOWN_CHUNK = 512
MLP_PARTS = 16


def _mlp_kernel(xn_ref, xns_ref, rinv_ref, rinvs_ref, xres_ref, xsres_ref, wg_ref, wv_ref, wd_ref, dw_ref, db_ref,
                past_ref,
                out_ref, outs_ref, ffnp_ref, ffns_ref,
                g_ref, v_ref, act_ref, carry_ref, gs_ref, vs_ref, acts_ref):
    m = pl.program_id(0)
    s = pl.program_id(1)
    bm = MLP_BM
    w0 = dw_ref[0:1, :]
    w1 = dw_ref[1:2, :]
    w2 = dw_ref[2:3, :]
    bias = db_ref[...]
    cslot = jnp.where(s == 0, MLP_NF, s - 1)

    @pl.when(s == 0)
    def _():
        out_ref[...] = jnp.zeros(out_ref.shape, F32)
        g_ref[...] = jnp.zeros(g_ref.shape, F32)
        v_ref[...] = jnp.zeros(v_ref.shape, F32)
        act_ref[...] = jnp.zeros(act_ref.shape, BF16)

    @pl.when((s == 0) & (m == 0))
    def _():
        outs_ref[...] = xsres_ref[...]
        gs_ref[...] = jnp.zeros(gs_ref.shape, F32)
        vs_ref[...] = jnp.zeros(vs_ref.shape, F32)
        acts_ref[...] = jnp.zeros(acts_ref.shape, BF16)

    for c in range(D_MODEL // MLP_RES):
        @pl.when(s == c)
        def _(c=c):
            out_ref[:, c * MLP_RES:(c + 1) * MLP_RES] += xres_ref[...]

    def down(act, acc_ref):
        for nc in range(D_MODEL // MLP_DOWN_CHUNK):
            cs = slice(nc * MLP_DOWN_CHUNK, (nc + 1) * MLP_DOWN_CHUNK)
            acc_ref[:, cs] += _dot(act, wd_ref[:, cs].astype(BF16))

    def row_scale(r_ref):
        return jnp.tile(r_ref[...], (1, MLP_BF // 128))

    def up(x_ref, r_ref):
        g = None
        v = None
        for kc in range(4):
            ks = slice(kc * 1024, (kc + 1) * 1024)
            x = x_ref[:, ks]
            pg = _dot(x, wg_ref[ks, :].astype(BF16))
            pv = _dot(x, wv_ref[ks, :].astype(BF16))
            g = pg if g is None else g + pg
            v = pv if v is None else v + pv
        r = row_scale(r_ref)
        return g * r, v * r

    wslot = s % 2
    rslot = 1 - wslot

    def sample_step(do_up, do_epilogue):
        down(acts_ref[rslot], outs_ref)
        if do_epilogue:
            gs = gs_ref[...]
            c = w0 * past_ref[0] + w1 * past_ref[1] + w2 * gs + bias
            acts_ref[wslot] = (_silu(c) * vs_ref[...]).astype(BF16)
            ffns_ref[0] = past_ref[1]
            ffns_ref[1] = gs
        if do_up:
            g_new, v_new = up(xns_ref, rinvs_ref)
            gs_ref[...] = g_new
            vs_ref[...] = v_new

    def prompt_step(do_up, do_epilogue):
        if do_epilogue:
            @pl.when(m % (SEQ // bm) == 0)
            def _():
                g_ref[0:FFN_HALO, :] = jnp.zeros((FFN_HALO, MLP_BF), F32)

            @pl.when(m % (SEQ // bm) != 0)
            def _():
                g_ref[0:FFN_HALO, :] = carry_ref[cslot]

        if do_up and do_epilogue:
            parts = MLP_PARTS
            rows = bm // parts
            kw = D_MODEL // parts
            g_new = v_new = None
            tie = None
            for q in range(parts):
                x = xn_ref[:, q * kw:(q + 1) * kw]
                if tie is None:
                    wg = wg_ref[q * kw:(q + 1) * kw, :].astype(BF16)
                else:
                    wg = jnp.concatenate([(wg_ref[q * kw:q * kw + 16, :] + tie).astype(BF16),
                                          wg_ref[q * kw + 16:(q + 1) * kw, :].astype(BF16)], axis=0)
                pg = _dot(x, wg)
                pv = _dot(x, wv_ref[q * kw:(q + 1) * kw, :].astype(BF16))
                g_new = pg if g_new is None else g_new + pg
                v_new = pv if v_new is None else v_new + pv
                lo = FFN_HALO + q * rows
                g = g_ref[lo:lo + rows, :]
                g1 = g_ref[lo - 1:lo - 1 + rows, :]
                g2 = g_ref[lo - 2:lo - 2 + rows, :]
                c = w0 * g2 + w1 * g1 + w2 * g + bias
                a = _silu(c) * v_ref[q * rows:(q + 1) * rows, :]
                act_ref[wslot, q * rows:(q + 1) * rows, :] = a.astype(BF16)
                t = a[:, 0:128] + a[:, 128:256]
                t = jnp.sum(t.reshape(rows // 8, 8, 128), axis=0)
                tie = jnp.tile(t * 0.0, (2, 2))
            ffnp_ref[...] = g_ref[FFN_HALO + bm - 2:FFN_HALO + bm, :]
            carry_ref[cslot] = g_ref[bm:bm + FFN_HALO, :]
            down(act_ref[rslot], out_ref)
            r = row_scale(rinv_ref)
            g_ref[FFN_HALO:FFN_HALO + bm, :] = g_new * r
            v_ref[...] = v_new * r
            return
        if do_up:
            g_new, v_new = up(xn_ref, rinv_ref)
        down(act_ref[rslot], out_ref)
        if do_epilogue:
            g = g_ref[FFN_HALO:FFN_HALO + bm, :]
            g1 = g_ref[FFN_HALO - 1:FFN_HALO - 1 + bm, :]
            g2 = g_ref[FFN_HALO - 2:FFN_HALO - 2 + bm, :]
            c = w0 * g2 + w1 * g1 + w2 * g + bias
            act_ref[wslot] = (_silu(c) * v_ref[...]).astype(BF16)
            ffnp_ref[...] = g_ref[FFN_HALO + bm - 2:FFN_HALO + bm, :]
            carry_ref[cslot] = g_ref[bm:bm + FFN_HALO, :]
        if do_up:
            g_ref[FFN_HALO:FFN_HALO + bm, :] = g_new
            v_ref[...] = v_new

    for cond, do_up, do_epilogue in ((s < MLP_NF, True, True), (s == MLP_NF, False, True),
                                     (s == MLP_NF + 1, False, False)):
        @pl.when(cond & (m == 0))
        def _(do_up=do_up, do_epilogue=do_epilogue):
            sample_step(do_up, do_epilogue)

        @pl.when(cond)
        def _(do_up=do_up, do_epilogue=do_epilogue):
            prompt_step(do_up, do_epilogue)


def _mlp(xnp, xns, rinvp, rinvs, xp, xs, w_up, w_down, layer, dw_w, dw_b, past_t):
    m, d = xnp.shape
    sb = xns.shape[0]
    bm, bf, nf = MLP_BM, MLP_BF, MLP_NF
    tiles_per_seq = SEQ // bm
    nres = d // MLP_RES
    one = pl.Buffered(1)

    def prev(s):
        return jnp.clip(s - 1, 0, nf - 1)

    def prev2(s):
        return jnp.clip(s - 2, 0, nf - 1)

    in_specs = [pl.BlockSpec((bm, d), lambda i, s: (i, 0), pipeline_mode=one),
                pl.BlockSpec((sb, d), lambda i, s: (0, 0)),
                pl.BlockSpec((bm, 128), lambda i, s: (i, 0)),
                pl.BlockSpec((sb, 128), lambda i, s: (0, 0)),
                pl.BlockSpec((bm, MLP_RES), lambda i, s: (i, jnp.minimum(s, nres - 1))),
                pl.BlockSpec((sb, d), lambda i, s: (0, 0)),
                pl.BlockSpec((None, d, bf), lambda i, s: (layer, 0, jnp.minimum(s, nf - 1))),
                pl.BlockSpec((None, d, bf), lambda i, s: (layer, 0, jnp.minimum(s, nf - 1) + nf)),
                pl.BlockSpec((None, bf, d), lambda i, s: (layer, prev2(s), 0)),
                pl.BlockSpec((FFN_CONV_WIDTH, bf), lambda i, s: (0, prev(s))),
                pl.BlockSpec((1, bf), lambda i, s: (0, prev(s))),
                pl.BlockSpec((2, sb, bf), lambda i, s: (0, 0, prev(s)))]
    out_specs = [pl.BlockSpec((bm, d), lambda i, s: (i, 0), pipeline_mode=one),
                 pl.BlockSpec((sb, d), lambda i, s: (0, 0)),
                 pl.BlockSpec((None, 2, bf), lambda i, s: (i, 0, prev(s))),
                 pl.BlockSpec((2, sb, bf), lambda i, s: (0, 0, jnp.where(i == 0, prev(s), nf - 1)))]
    out_shape = [jax.ShapeDtypeStruct((m, d), F32),
                 jax.ShapeDtypeStruct((sb, d), F32),
                 jax.ShapeDtypeStruct((m // bm, 2, D_FF), F32),
                 jax.ShapeDtypeStruct((2, sb, D_FF), F32)]
    scratch = [pltpu.VMEM((FFN_HALO + bm, bf), F32), pltpu.VMEM((bm, bf), F32), pltpu.VMEM((2, bm, bf), BF16),
               pltpu.VMEM((nf + 1, FFN_HALO, bf), F32),
               pltpu.VMEM((sb, bf), F32), pltpu.VMEM((sb, bf), F32), pltpu.VMEM((2, sb, bf), BF16)]
    yp, ys, tails, ffn_s = pl.pallas_call(
        _mlp_kernel,
        grid=(m // bm, nf + 2),
        in_specs=in_specs, out_specs=out_specs, out_shape=out_shape, scratch_shapes=scratch,
        compiler_params=_params(("arbitrary", "arbitrary")),
        name="conv_ffn",
    )(xnp, xns, rinvp, rinvs, xp, xs, w_up, w_up, w_down, dw_w, dw_b.reshape(1, D_FF), past_t)
    return yp, ys, tails[tiles_per_seq - 1::tiles_per_seq], ffn_s


def _mixer_prompt_kernel(h_ref, dmask_ref, qdec_ref, kdec_ref, cdec_ref, gn_ref, qg_ref, kg_ref,
                         sinkb_ref, biasc_ref, biasp_ref,
                         cw_ref, cb_ref, lng_ref, lnb_ref, pw2_ref, poolw_ref, pscale_ref,
                         mix_ref, ret_ref, kwin_ref, vwin_ref, convp_ref, poolp_ref,
                         s_ref, kprev_ref, vprev_ref, uwin_ref, pwin_ref, pw2bf_ref, poolwbf_ref):
    b = pl.program_id(0)
    t = pl.program_id(1)

    @pl.when((b == 0) & (t == 0))
    def _():
        pw2bf_ref[...] = pw2_ref[...].astype(BF16)
        poolwbf_ref[...] = poolw_ref[...].astype(BF16)

    @pl.when(t == 0)
    def _():
        s_ref[...] = jnp.zeros(s_ref.shape, F32)
        kprev_ref[...] = jnp.zeros(kprev_ref.shape, F32)
        vprev_ref[...] = jnp.zeros(vprev_ref.shape, F32)
        uwin_ref[0:CONV_HALO, :] = jnp.zeros((CONV_HALO, CONV_CH), F32)
        pwin_ref[0:POOL_HALO, :] = jnp.zeros((POOL_HALO, POOL_CH), F32)

    ret_out, ret_state = [], []
    for hh in range(RET_HEADS):
        lo = hh * RET_HEAD_DIM
        hi = lo + RET_HEAD_DIM
        q = h_ref[:, COL_RQ + lo:COL_RQ + hi]
        k = h_ref[:, COL_RK + lo:COL_RK + hi] * (RET_HEAD_DIM ** -0.5)
        v = h_ref[:, COL_RV + lo:COL_RV + hi]
        qb = q.astype(BF16)
        vb = v.astype(BF16)
        a = _dot_nt(qb, k.astype(BF16)) * dmask_ref[hh]
        s_old = s_ref[hh]
        o = _dot(a.astype(BF16), vb) + _dot(qb, s_old.astype(BF16)) * qdec_ref[hh]
        ret_state.append(s_old * cdec_ref[hh] + _dot_tn((k * kdec_ref[hh]).astype(BF16), vb))
        y = _center_norm(o) * gn_ref[:, lo:hi]
        gate = h_ref[:, COL_RG + lo:COL_RG + hi]
        ret_out.append((_silu(gate) * y).astype(mix_ref.dtype))
    mix_ref[:, 0:GROUP_WIDTH] = jnp.concatenate(ret_out, axis=-1)
    for hh in range(RET_HEADS):
        s_ref[hh] = ret_state[hh]
        ret_ref[hh] = ret_state[hh]

    lane = lax.broadcasted_iota(jnp.int32, (1, 128), 1)
    half_mask = (lane < ATT_HEAD_DIM, lane >= ATT_HEAD_DIM)

    def pair_rms(x, gain):
        sq = x * x
        ms = [jnp.sum(jnp.where(hm, sq, 0.0), axis=-1, keepdims=True) * (1.0 / ATT_HEAD_DIM) for hm in half_mask]
        r = jnp.where(half_mask[0], lax.rsqrt(ms[0] + NORM_EPS), lax.rsqrt(ms[1] + NORM_EPS))
        return x * r * gain

    kn = pair_rms(h_ref[:, COL_AK:COL_AK + 128], kg_ref[...])
    vc = h_ref[:, COL_AV:COL_AV + 128]
    kwin_ref[...] = kn
    vwin_ref[...] = vc
    kp = kprev_ref[...]
    vp = vprev_ref[...]
    kprev_ref[...] = kn
    vprev_ref[...] = vc
    swapped = {id(x): pltpu.roll(x, ATT_HEAD_DIM, axis=1) for x in (kn, vc, kp, vp)}

    def half_of(x, src, dst):
        y = x if src == dst else swapped[id(x)]
        return jnp.where(half_mask[dst], y, 0.0).astype(BF16)

    scale = ATT_HEAD_DIM ** -0.5
    att_out = []
    for g in range(ATT_KV_HEADS):
        q0 = COL_AQ + g * ATT_GROUP * ATT_HEAD_DIM
        qst = jnp.concatenate([h_ref[:, q0 + p * 128:q0 + (p + 1) * 128] for p in range(ATT_GROUP // 2)], axis=0)
        qn = pair_rms(qst, qg_ref[...]).astype(BF16)
        o = None
        for par in range(2):
            s_cur = _dot_nt(qn, half_of(kn, g, par)) * scale + biasc_ref[g, par]
            s_prev = _dot_nt(qn, half_of(kp, g, par)) * scale + biasp_ref[g, par]
            s_prev = jnp.where(t > 0, s_prev, -jnp.inf)
            sink = sinkb_ref[g, par]
            mx = jnp.max(jnp.maximum(jnp.maximum(s_cur, s_prev), sink), axis=-1, keepdims=True)
            e_cur = jnp.exp(s_cur - mx)
            e_prev = jnp.exp(s_prev - mx)
            den = jnp.sum(e_cur + e_prev, axis=-1, keepdims=True) + jnp.exp(sink[:, 0:1] - mx)
            inv = 1.0 / den
            od = (_dot((e_cur * inv).astype(BF16), half_of(vc, g, par))
                  + _dot((e_prev * inv).astype(BF16), half_of(vp, g, par)))
            o = od if o is None else o + od
        att_out += [o[p * 128:(p + 1) * 128, :].astype(mix_ref.dtype) for p in range(ATT_GROUP // 2)]
    mix_ref[:, GROUP_WIDTH:2 * GROUP_WIDTH] = jnp.concatenate(att_out, axis=-1)

    u = h_ref[:, COL_CA:COL_CA + CONV_CH] * _sigmoid(h_ref[:, COL_CG:COL_CG + CONV_CH])
    uwin_ref[CONV_HALO:CONV_HALO + TILE, :] = u
    wrows = TILE + 8
    cols = []
    for cc in range(CONV_CH // 128):
        cs = slice(cc * 128, (cc + 1) * 128)
        acc = None
        for k in range(8):
            part = None
            for a in range(4):
                j = CONV_WIDTH - 1 - (8 * a + k)
                if j < 0:
                    continue
                start = CONV_HALO - 8 - 8 * a
                term = cw_ref[j:j + 1, cs] * uwin_ref[start:start + wrows, cs]
                part = term if part is None else part + term
            part = part[8 - k:8 - k + TILE, :]
            acc = part if acc is None else acc + part
        cols.append(acc)
    c = jnp.concatenate(cols, axis=-1) + cb_ref[...]
    c = _silu(_center_norm(c) * lng_ref[...] + lnb_ref[...])
    mix_ref[:, 2 * GROUP_WIDTH:3 * GROUP_WIDTH] = _dot(c.astype(BF16), pw2bf_ref[...]).astype(mix_ref.dtype)
    convp_ref[...] = uwin_ref[CONV_HALO + TILE - (CONV_WIDTH - 1):CONV_HALO + TILE, :]
    uwin_ref[0:CONV_HALO, :] = uwin_ref[TILE:TILE + CONV_HALO, :]

    pin = h_ref[:, COL_PIN:COL_PIN + POOL_CH]
    pwin_ref[POOL_HALO:POOL_HALO + TILE, :] = pin
    pos = (t * TILE + lax.broadcasted_iota(jnp.int32, (TILE, 1), 0) + 1).astype(F32)
    pool_out = []
    for gi, w in enumerate(POOL_WINDOWS):
        cs = slice(gi * POOL_GROUP, (gi + 1) * POOL_GROUP)
        xg = pin[:, cs]
        sm = xg
        for d in range(1, w):
            sm = sm + pwin_ref[POOL_HALO - d:POOL_HALO - d + TILE, cs]
        cnt = jnp.minimum(float(w), pos)
        dd = sm / cnt - xg
        y = _dot(dd.astype(BF16), poolwbf_ref[gi]) * pscale_ref[:, cs]
        pool_out.append(y.astype(mix_ref.dtype))
    mix_ref[:, 3 * GROUP_WIDTH:4 * GROUP_WIDTH] = jnp.concatenate(pool_out, axis=-1)
    poolp_ref[...] = pwin_ref[POOL_HALO + TILE - (POOL_MAX - 1):POOL_HALO + TILE, :]
    pwin_ref[0:POOL_HALO, :] = pwin_ref[TILE:TILE + POOL_HALO, :]


def _full(shape):
    nd = len(shape)
    return pl.BlockSpec(shape, lambda b, t: (0,) * nd)


def _stack_heads(per_head):
    x = per_head.reshape((ATT_KV_HEADS, ATT_GROUP // 2, 2) + per_head.shape[1:])
    x = jnp.swapaxes(x, 1, 2)
    return x.reshape((ATT_KV_HEADS, 2, (ATT_GROUP // 2) * per_head.shape[1]) + per_head.shape[2:])


def _attention_bias_tables():
    slopes = jnp.exp2(-8.0 * jnp.arange(1, ATT_HEADS + 1, dtype=F32) / ATT_HEADS)[:, None, None]
    idx = jnp.arange(TILE)
    rel = (idx[:, None] - idx[None, :])
    dist = rel.astype(F32)[None]
    cur = jnp.where((rel >= 0)[None], -slopes * dist, -jnp.inf)
    prev = jnp.where((rel <= 0)[None], -slopes * (dist + float(TILE)), -jnp.inf)
    return _stack_heads(cur), _stack_heads(prev)


def _mixer_prompt(h, tables, bias, gn, qg, kg, sinks, cw, cb, lng, lnb, pw2, poolw, pscale):
    dmask, qdec, kdec, cdec = tables
    biasc, biasp = bias
    sinkb = _stack_heads(jnp.broadcast_to(sinks[:, None, None], (ATT_HEADS, TILE, 128)))
    qg = jnp.tile(qg, 2)
    kg = jnp.tile(kg, 2)
    in_specs = [pl.BlockSpec((TILE, IN_WIDTH), lambda b, t: (b * TILES_PER_SEQ + t, 0)),
                _full(dmask.shape), _full(qdec.shape), _full(kdec.shape), _full(cdec.shape),
                _full((1, GROUP_WIDTH)), _full((1, 128)), _full((1, 128)),
                _full(sinkb.shape), _full(biasc.shape), _full(biasp.shape),
                _full((CONV_WIDTH, CONV_CH)), _full((1, CONV_CH)), _full((1, CONV_CH)), _full((1, CONV_CH)),
                _full((CONV_CH, GROUP_WIDTH)), _full((4, POOL_GROUP, POOL_GROUP)), _full((1, POOL_CH))]
    out_specs = [pl.BlockSpec((TILE, 4 * GROUP_WIDTH), lambda b, t: (b * TILES_PER_SEQ + t, 0)),
                 pl.BlockSpec((None, RET_HEADS, RET_HEAD_DIM, RET_HEAD_DIM), lambda b, t: (b, 0, 0, 0)),
                 pl.BlockSpec((None, WINDOW, 128), lambda b, t: (b, 0, 0)),
                 pl.BlockSpec((None, WINDOW, 128), lambda b, t: (b, 0, 0)),
                 pl.BlockSpec((None, CONV_WIDTH - 1, CONV_CH), lambda b, t: (b, 0, 0)),
                 pl.BlockSpec((None, POOL_MAX - 1, POOL_CH), lambda b, t: (b, 0, 0))]
    out_shape = [jax.ShapeDtypeStruct((NTOK, 4 * GROUP_WIDTH), BF16),
                 jax.ShapeDtypeStruct((BATCH, RET_HEADS, RET_HEAD_DIM, RET_HEAD_DIM), F32),
                 jax.ShapeDtypeStruct((BATCH, WINDOW, 128), F32),
                 jax.ShapeDtypeStruct((BATCH, WINDOW, 128), F32),
                 jax.ShapeDtypeStruct((BATCH, CONV_WIDTH - 1, CONV_CH), F32),
                 jax.ShapeDtypeStruct((BATCH, POOL_MAX - 1, POOL_CH), F32)]
    scratch = [pltpu.VMEM((RET_HEADS, RET_HEAD_DIM, RET_HEAD_DIM), F32),
               pltpu.VMEM((TILE, 128), F32), pltpu.VMEM((TILE, 128), F32),
               pltpu.VMEM((CONV_HALO + TILE, CONV_CH), F32),
               pltpu.VMEM((POOL_HALO + TILE, POOL_CH), F32),
               pltpu.VMEM((CONV_CH, GROUP_WIDTH), BF16),
               pltpu.VMEM((4, POOL_GROUP, POOL_GROUP), BF16)]
    return pl.pallas_call(
        _mixer_prompt_kernel,
        grid=(BATCH, TILES_PER_SEQ),
        in_specs=in_specs, out_specs=out_specs, out_shape=out_shape, scratch_shapes=scratch,
        compiler_params=_params(("arbitrary", "arbitrary")),
        name="mixer_prompt",
    )(h, dmask, qdec, kdec, cdec, gn.reshape(1, -1), qg.reshape(1, -1), kg.reshape(1, -1), sinkb, biasc, biasp,
      cw, cb.reshape(1, -1), lng.reshape(1, -1), lnb.reshape(1, -1), pw2, poolw, pscale.reshape(1, -1))


def _mixer_sample_kernel(row_ref, hs_ref, s0_ref, ck_ref, cv_ref, cst_ref, pst_ref, gdec_ref, slope_ref,
                         gn_ref, qg_ref, kg_ref, sink_ref,
                         cw_ref, cb_ref, lng_ref, lnb_ref, pw2_ref, poolw_ref, pscale_ref,
                         ra_ref, mix_ref, ret_ref, kwin_ref, vwin_ref, convs_ref, pools_ref):
    b = pl.program_id(0)
    nb = pl.num_programs(0)

    for hh in range(RET_HEADS):
        lo = hh * RET_HEAD_DIM
        hi = lo + RET_HEAD_DIM
        q = row_ref[:, COL_RQ + lo:COL_RQ + hi]
        k = row_ref[:, COL_RK + lo:COL_RK + hi] * (RET_HEAD_DIM ** -0.5)
        v = row_ref[:, COL_RV + lo:COL_RV + hi]
        gate = row_ref[:, COL_RG + lo:COL_RG + hi]
        gdec = gdec_ref[hh]
        s_old = s0_ref[hh]
        q8 = jnp.broadcast_to(q, (8, RET_HEAD_DIM)).astype(BF16)
        qs = _dot(q8, s_old.astype(BF16))[0:1, :]
        a = jnp.sum(q * k, axis=-1, keepdims=True)
        o = a * v + qs * gdec
        k8 = jnp.where(lax.broadcasted_iota(jnp.int32, (8, RET_HEAD_DIM), 0) == 0, k, 0.0).astype(BF16)
        v8 = jnp.broadcast_to(v, (8, RET_HEAD_DIM)).astype(BF16)
        ret_ref[hh] = s_old * gdec + _dot_tn(k8, v8)
        y = _center_norm(o) * gn_ref[:, lo:hi]
        ra_ref[:, lo:hi] = _silu(gate) * y

    knew = row_ref[:, COL_AK:COL_AK + 128]
    vnew = row_ref[:, COL_AV:COL_AV + 128]
    scale = ATT_HEAD_DIM ** -0.5
    dist = (WINDOW - lax.broadcasted_iota(jnp.int32, (1, WINDOW), 1)).astype(F32)
    kn_parts = []
    for g in range(ATT_KV_HEADS):
        sl = slice(g * ATT_HEAD_DIM, (g + 1) * ATT_HEAD_DIM)
        kng = _rms(knew[:, sl], kg_ref[...])
        kn_parts.append(kng)
        vng = vnew[:, sl]
        q0 = COL_AQ + g * ATT_GROUP * ATT_HEAD_DIM
        qs8 = jnp.concatenate(
            [row_ref[:, q0 + i * ATT_HEAD_DIM:q0 + (i + 1) * ATT_HEAD_DIM] for i in range(ATT_GROUP)], axis=0)
        qn = _rms(qs8, qg_ref[...])
        slope = slope_ref[g]
        sink = sink_ref[g]
        s_past = _dot_nt(qn.astype(BF16), ck_ref[:, sl].astype(BF16)) * scale - slope * dist
        s_new = jnp.sum(qn * kng, axis=-1, keepdims=True) * scale
        mx = jnp.maximum(jnp.maximum(jnp.max(s_past, axis=-1, keepdims=True), s_new), sink)
        e_past = jnp.exp(s_past - mx)
        e_new = jnp.exp(s_new - mx)
        den = jnp.sum(e_past, axis=-1, keepdims=True) + e_new + jnp.exp(sink - mx)
        inv = 1.0 / den
        o = _dot((e_past * inv).astype(BF16), cv_ref[:, sl].astype(BF16)) + (e_new * inv) * vng
        for i in range(ATT_GROUP):
            c0 = GROUP_WIDTH + (g * ATT_GROUP + i) * ATT_HEAD_DIM
            ra_ref[:, c0:c0 + ATT_HEAD_DIM] = o[i:i + 1, :]
    kwin_ref[0:WINDOW - 1, :] = ck_ref[1:WINDOW, :]
    kwin_ref[WINDOW - 1:WINDOW, :] = jnp.concatenate(kn_parts, axis=-1)
    vwin_ref[0:WINDOW - 1, :] = cv_ref[1:WINDOW, :]
    vwin_ref[WINDOW - 1:WINDOW, :] = vnew

    @pl.when(b == nb - 1)
    def _():
        u = hs_ref[:, COL_CA:COL_CA + CONV_CH] * _sigmoid(hs_ref[:, COL_CG:COL_CG + CONV_CH])
        c = cw_ref[CONV_WIDTH - 1:CONV_WIDTH, :] * u
        for j in range(CONV_WIDTH - 1):
            c = c + cw_ref[j:j + 1, :] * cst_ref[j]
        c = c + cb_ref[...]
        c = _silu(_center_norm(c) * lng_ref[...] + lnb_ref[...])
        mix_ref[:, 0:GROUP_WIDTH] = _dot(c.astype(BF16), pw2_ref[...].astype(BF16)).astype(mix_ref.dtype)
        for j in range(CONV_WIDTH - 2):
            convs_ref[j] = cst_ref[j + 1]
        convs_ref[CONV_WIDTH - 2] = u

        pin = hs_ref[:, COL_PIN:COL_PIN + POOL_CH]
        npast = POOL_MAX - 1
        for gi, w in enumerate(POOL_WINDOWS):
            cs = slice(gi * POOL_GROUP, (gi + 1) * POOL_GROUP)
            xg = pin[:, cs]
            sm = xg
            for d in range(1, w):
                sm = sm + pst_ref[npast - d, :, cs]
            dd = sm / float(w) - xg
            y = _dot(dd.astype(BF16), poolw_ref[gi].astype(BF16)) * pscale_ref[:, cs]
            mix_ref[:, GROUP_WIDTH + gi * POOL_GROUP:GROUP_WIDTH + (gi + 1) * POOL_GROUP] = y.astype(mix_ref.dtype)
        for j in range(npast - 1):
            pools_ref[j] = pst_ref[j + 1]
        pools_ref[npast - 1] = pin


def _mixer_sample(layer, hs, state_ret, cache_k, cache_v, cst, pst, gdec, slopes,
                  gn, qg, kg, sinks, cw, cb, lng, lnb, pw2, poolw, pscale):
    nb = DEC_BATCH

    def full(shape):
        nd = len(shape)
        return pl.BlockSpec(shape, lambda b: (0,) * nd)

    in_specs = [pl.BlockSpec((None, 1, IN_WIDTH), lambda b: (b, 0, 0)),
                full((nb, IN_WIDTH)),
                pl.BlockSpec((None, None, RET_HEADS, RET_HEAD_DIM, RET_HEAD_DIM), lambda b: (layer, b, 0, 0, 0)),
                pl.BlockSpec((None, None, WINDOW, 128), lambda b: (layer, b, 0, 0)),
                pl.BlockSpec((None, None, WINDOW, 128), lambda b: (layer, b, 0, 0)),
                full((CONV_WIDTH - 1, nb, CONV_CH)), full((POOL_MAX - 1, nb, POOL_CH)),
                full((RET_HEADS, 1, RET_HEAD_DIM)), full((ATT_KV_HEADS, ATT_GROUP, 1)),
                full((1, GROUP_WIDTH)), full((1, ATT_HEAD_DIM)), full((1, ATT_HEAD_DIM)),
                full((ATT_KV_HEADS, ATT_GROUP, 1)),
                full((CONV_WIDTH, CONV_CH)), full((1, CONV_CH)), full((1, CONV_CH)), full((1, CONV_CH)),
                full((CONV_CH, GROUP_WIDTH)), full((4, POOL_GROUP, POOL_GROUP)), full((1, POOL_CH))]
    out_specs = [pl.BlockSpec((None, 1, 2 * GROUP_WIDTH), lambda b: (b, 0, 0)),
                 full((nb, 2 * GROUP_WIDTH)),
                 pl.BlockSpec((None, RET_HEADS, RET_HEAD_DIM, RET_HEAD_DIM), lambda b: (b, 0, 0, 0)),
                 pl.BlockSpec((None, WINDOW, 128), lambda b: (b, 0, 0)),
                 pl.BlockSpec((None, WINDOW, 128), lambda b: (b, 0, 0)),
                 full((CONV_WIDTH - 1, nb, CONV_CH)), full((POOL_MAX - 1, nb, POOL_CH))]
    out_shape = [jax.ShapeDtypeStruct((nb, 1, 2 * GROUP_WIDTH), F32),
                 jax.ShapeDtypeStruct((nb, 2 * GROUP_WIDTH), BF16),
                 jax.ShapeDtypeStruct((nb, RET_HEADS, RET_HEAD_DIM, RET_HEAD_DIM), F32),
                 jax.ShapeDtypeStruct((nb, WINDOW, 128), F32),
                 jax.ShapeDtypeStruct((nb, WINDOW, 128), F32),
                 jax.ShapeDtypeStruct((CONV_WIDTH - 1, nb, CONV_CH), F32),
                 jax.ShapeDtypeStruct((POOL_MAX - 1, nb, POOL_CH), F32)]
    ra, cp, *states = pl.pallas_call(
        _mixer_sample_kernel,
        grid=(nb,),
        in_specs=in_specs, out_specs=out_specs, out_shape=out_shape,
        compiler_params=_params(("arbitrary",)),
        name="mixer_sample",
    )(hs.reshape(nb, 1, IN_WIDTH), hs, state_ret, cache_k, cache_v, cst, pst, gdec, slopes,
      gn.reshape(1, -1), qg.reshape(1, -1), kg.reshape(1, -1), sinks.reshape(ATT_KV_HEADS, ATT_GROUP, 1),
      cw, cb.reshape(1, -1), lng.reshape(1, -1), lnb.reshape(1, -1), pw2, poolw, pscale.reshape(1, -1))
    mix = jnp.concatenate([ra.reshape(nb, 2 * GROUP_WIDTH).astype(BF16), cp], axis=-1)
    return (mix, *states)


def _retention_tables():
    c = RET_CHUNK
    log_g = jnp.log1p(-jnp.exp2(-5.0 - jnp.arange(RET_HEADS, dtype=F32)))
    idx = jnp.arange(c, dtype=F32)
    rel = idx[:, None] - idx[None, :]
    dmask = jnp.where(rel >= 0, jnp.exp(log_g[:, None, None] * jnp.maximum(rel, 0.0)), 0.0)
    ones = jnp.ones((1, 1, RET_HEAD_DIM), F32)
    qdec = jnp.exp(log_g[:, None] * (idx + 1.0))[:, :, None] * ones
    kdec = jnp.exp(log_g[:, None] * (c - 1.0 - idx))[:, :, None] * ones
    cdec = jnp.exp(log_g * c)[:, None, None] * jnp.ones((1, RET_HEAD_DIM, RET_HEAD_DIM), F32)
    gdec = jnp.exp(log_g)[:, None, None] * ones
    return (dmask, qdec, kdec, cdec), gdec


def kernel(x_prompt, x_sample, state_ret, cache_k_win, cache_v_win, state_conv, state_pool, state_ffn,
           norm1_g, w_in, ret_gn_g, q_norm_g, k_norm_g, att_sinks, conv_dw_w, conv_dw_b, conv_ln_g, conv_ln_b,
           conv_pw2, pool_w, pool_scale, w_out, norm2_g, ffn_up, ffn_dw_w, ffn_dw_b, ffn_down):
    tables, gdec = _retention_tables()
    bias = _attention_bias_tables()
    slopes =jnp.exp2(-8.0 * jnp.arange(1, ATT_HEADS + 1, dtype=F32) / ATT_HEADS).reshape(ATT_KV_HEADS, ATT_GROUP, 1)
    cache_k = cache_k_win.reshape(DEPTH, DEC_BATCH, WINDOW, 128)
    cache_v = cache_v_win.reshape(DEPTH, DEC_BATCH, WINDOW, 128)

    xp = x_prompt.reshape(NTOK, D_MODEL)
    xs = x_sample.reshape(DEC_BATCH, D_MODEL)
    outs_p = [[] for _ in range(6)]
    outs_s = [[] for _ in range(6)]
    for l in range(DEPTH):
        xnp = _rmsnorm(xp, norm1_g[l], 256)
        xns = _rmsnorm(xs, norm1_g[l], DEC_BATCH)
        hp, hs = _matmul(xnp, xns, w_in, l, 1024, 768, name="w_in")
        mixp, ret_p, kwin_p, vwin_p, conv_p, pool_p = _mixer_prompt(
            hp, tables, bias, ret_gn_g[l], q_norm_g[l], k_norm_g[l], att_sinks[l], conv_dw_w[l], conv_dw_b[l],
            conv_ln_g[l], conv_ln_b[l], conv_pw2[l], pool_w[l], pool_scale[l])
        cst = jnp.swapaxes(state_conv[l], 0, 1)
        pst = jnp.swapaxes(state_pool[l], 0, 1)
        mixs, ret_s, kwin_s, vwin_s, conv_s, pool_s = _mixer_sample(
            l, hs, state_ret, cache_k, cache_v, cst, pst, gdec, slopes,
            ret_gn_g[l], q_norm_g[l], k_norm_g[l], att_sinks[l], conv_dw_w[l], conv_dw_b[l],
            conv_ln_g[l], conv_ln_b[l], conv_pw2[l], pool_w[l], pool_scale[l])
        xp, xs, xgp, xgs, rinvp, rinvs = _w_out(mixp, mixs, w_out, l, xp, xs, norm2_g[l], 1024, 512)
        fst = jnp.swapaxes(state_ffn[l], 0, 1)
        xp, xs, ffn_p, ffn_s = _mlp(xgp, xgs, rinvp, rinvs, xp, xs, ffn_up, ffn_down, l,
                                    ffn_dw_w[l], ffn_dw_b[l], fst)
        for lst, v in zip(outs_p, (ret_p, kwin_p.reshape(BATCH, WINDOW, ATT_KV_HEADS, ATT_HEAD_DIM),
                                   vwin_p.reshape(BATCH, WINDOW, ATT_KV_HEADS, ATT_HEAD_DIM), conv_p, pool_p, ffn_p)):
            lst.append(v)
        for lst, v in zip(outs_s, (ret_s, kwin_s.reshape(DEC_BATCH, WINDOW, ATT_KV_HEADS, ATT_HEAD_DIM),
                                   vwin_s.reshape(DEC_BATCH, WINDOW, ATT_KV_HEADS, ATT_HEAD_DIM),
                                   jnp.swapaxes(conv_s, 0, 1), jnp.swapaxes(pool_s, 0, 1),
                                   jnp.swapaxes(ffn_s, 0, 1))):
            lst.append(v)
    y_prompt = xp.reshape(BATCH, SEQ, D_MODEL)
    y_sample = xs.reshape(DEC_BATCH, 1, D_MODEL)
    return (y_prompt, y_sample, *[jnp.stack(v) for v in outs_p], *[jnp.stack(v) for v in outs_s])
```
